```python
import jax, jax.numpy as jnp
from jax import lax
import numpy as np

D_MODEL = 2048
BATCH = 4
SEQ = 8192
DEPTH = 4

CHUNK = 64
D_MIX = D_MODEL
ATTN_HEAD_DIM = 128
ATTN_HEADS = (D_MIX // 2) // ATTN_HEAD_DIM
D_ATTN = ATTN_HEADS * ATTN_HEAD_DIM
D_CONF = D_MIX // 4
D_SCONV = D_MIX - D_ATTN - D_CONF
CONF_WIDTH = 31
SCONV_WIDTH = 3
FFN_WIDTH = 3
D_FF = 5632
Q_BLOCK = 128
N_ADA = 6
RMS_EPS = 1e-6
LN_EPS = 1e-5
IN_COLS = 3 * D_ATTN + ATTN_HEADS + 2 * D_CONF + 3 * D_SCONV

kernel_name = "hybrid_fox_conformer_shortconv_trunk"


def rms_norm(x, g):
    xf = x.astype(jnp.float32)
    y = xf * lax.rsqrt(jnp.mean(xf * xf, axis=-1, keepdims=True) + RMS_EPS)
    return (y * g.astype(jnp.float32)).astype(x.dtype)


def layer_norm(x, g, b):
    xf = x.astype(jnp.float32)
    mu = jnp.mean(xf, axis=-1, keepdims=True)
    xc = xf - mu
    y = xc * lax.rsqrt(jnp.mean(xc * xc, axis=-1, keepdims=True) + LN_EPS)
    return (y * g.astype(jnp.float32) + b.astype(jnp.float32)).astype(x.dtype)


def modulate(h, shift, scale):
    return h * (1 + scale[:, None, :]) + shift[:, None, :]


def causal_dwconv(x, w, b=None):
    K, C = w.shape
    xp = jnp.pad(x, ((0, 0), (K - 1, 0), (0, 0)))
    y = lax.conv_general_dilated(xp, w[:, None, :], window_strides=(1,), padding='VALID',
                                 dimension_numbers=('NWC', 'WIO', 'NWC'),
                                 feature_group_count=C)
    if b is not None:
        y = y + b
    return y


def forgetting_attention(q, k, v, log_f):
    B, S, H, Dh = q.shape
    nb = S // Q_BLOCK
    F = jnp.cumsum(log_f, axis=1)
    Fk = jnp.transpose(F, (0, 2, 1))[:, :, None, :]
    qb = jnp.transpose(q.reshape(B, nb, Q_BLOCK, H, Dh), (1, 0, 2, 3, 4))
    Fqb = jnp.transpose(F.reshape(B, nb, Q_BLOCK, H), (1, 0, 3, 2))
    qpos = jnp.arange(S).reshape(nb, Q_BLOCK)
    kpos = jnp.arange(S)
    scale = Dh ** -0.5

    def one_block(args):
        q_blk, Fq_blk, qp = args
        s = jnp.einsum('bqhd,bkhd->bhqk', q_blk, k,
                       preferred_element_type=jnp.float32) * scale
        s = s + Fq_blk[..., None] - Fk
        mask = kpos[None, :] <= qp[:, None]
        s = jnp.where(mask, s, -jnp.inf)
        p = jax.nn.softmax(s, axis=-1)
        return jnp.einsum('bhqk,bkhd->bqhd', p.astype(v.dtype), v)

    out = lax.map(one_block, (qb, Fqb, qpos))
    return jnp.transpose(out, (1, 0, 2, 3, 4)).reshape(B, S, H * Dh)


def setup_inputs(seed: int = 0) -> dict:
    key = jax.random.key(seed)
    ks = jax.random.split(key, 24)
    L, D = DEPTH, D_MODEL
    nrm = jax.random.normal
    x = nrm(ks[0], (BATCH, SEQ, D), jnp.float32)
    c = nrm(ks[1], (BATCH, D), jnp.float32)
    ada_w = nrm(ks[2], (L, D, N_ADA * D), jnp.float32) * (0.5 * D ** -0.5)
    ada_b = nrm(ks[3], (L, N_ADA * D), jnp.float32) * 0.02
    mix_norm_g = 1.0 + 0.05 * nrm(ks[4], (L, D), jnp.float32)
    w_in = nrm(ks[5], (L, D, IN_COLS), jnp.float32) * D ** -0.5
    b_forget = jax.random.uniform(ks[6], (L, ATTN_HEADS), jnp.float32, 1.0, 4.0)
    conf_dw_w = nrm(ks[7], (L, CONF_WIDTH, D_CONF), jnp.float32) * CONF_WIDTH ** -0.5
    conf_dw_b = nrm(ks[8], (L, D_CONF), jnp.float32) * 0.02
    conf_ln_g = 1.0 + 0.05 * nrm(ks[9], (L, D_CONF), jnp.float32)
    conf_ln_b = nrm(ks[10], (L, D_CONF), jnp.float32) * 0.02
    sc_dw_w = nrm(ks[11], (L, SCONV_WIDTH, D_SCONV), jnp.float32) * SCONV_WIDTH ** -0.5
    w_out = nrm(ks[12], (L, D_MIX, D), jnp.float32) * D_MIX ** -0.5
    ffn_norm_g = 1.0 + 0.05 * nrm(ks[13], (L, D), jnp.float32)
    w_up = nrm(ks[14], (L, D, 2 * D_FF), jnp.float32) * D ** -0.5
    ffn_dw_w = nrm(ks[15], (L, FFN_WIDTH, 2 * D_FF), jnp.float32) * FFN_WIDTH ** -0.5
    ffn_dw_b = nrm(ks[16], (L, 2 * D_FF), jnp.float32) * 0.02
    w_down = nrm(ks[17], (L, D_FF, D), jnp.float32) * D_FF ** -0.5
    final_norm_g = 1.0 + 0.05 * nrm(ks[18], (D,), jnp.float32)
    return {"x": x, "c": c, "ada_w": ada_w, "ada_b": ada_b, "mix_norm_g": mix_norm_g,
            "w_in": w_in, "b_forget": b_forget, "conf_dw_w": conf_dw_w, "conf_dw_b": conf_dw_b,
            "conf_ln_g": conf_ln_g, "conf_ln_b": conf_ln_b, "sc_dw_w": sc_dw_w, "w_out": w_out,
            "ffn_norm_g": ffn_norm_g, "w_up": w_up, "ffn_dw_w": ffn_dw_w, "ffn_dw_b": ffn_dw_b,
            "w_down": w_down, "final_norm_g": final_norm_g}


def reference(x, c, ada_w, ada_b, mix_norm_g, w_in, b_forget, conf_dw_w, conf_dw_b,
              conf_ln_g, conf_ln_b, sc_dw_w, w_out, ffn_norm_g, w_up, ffn_dw_w, ffn_dw_b,
              w_down, final_norm_g):
    B, S, _ = x.shape
    split_at = list(np.cumsum([D_ATTN, D_ATTN, D_ATTN, ATTN_HEADS,
                               D_CONF, D_CONF, D_SCONV, D_SCONV]))
    c_act = jax.nn.silu(c)
    for l in range(DEPTH):
        ada = c_act @ ada_w[l] + ada_b[l]
        sh_m, sc_m, g_m, sh_f, sc_f, g_f = jnp.split(ada, N_ADA, axis=-1)

        h = modulate(rms_norm(x, mix_norm_g[l]), sh_m, sc_m)
        proj = h @ w_in[l]
        q, k, v, f_logit, cv, cg, s_x, s_b, s_c = jnp.split(proj, split_at, axis=-1)

        log_f = jax.nn.log_sigmoid((f_logit + b_forget[l]).astype(jnp.float32))
        attn = forgetting_attention(q.reshape(B, S, ATTN_HEADS, ATTN_HEAD_DIM),
                                    k.reshape(B, S, ATTN_HEADS, ATTN_HEAD_DIM),
                                    v.reshape(B, S, ATTN_HEADS, ATTN_HEAD_DIM), log_f)

        conf = cv * jax.nn.sigmoid(cg)
        conf = causal_dwconv(conf, conf_dw_w[l], conf_dw_b[l])
        conf = jax.nn.silu(layer_norm(conf, conf_ln_g[l], conf_ln_b[l]))

        sconv = s_b * causal_dwconv(s_c * s_x, sc_dw_w[l])

        mixed = jnp.concatenate([attn, conf, sconv], axis=-1) @ w_out[l]
        x = x + g_m[:, None, :] * mixed

        h = modulate(rms_norm(x, ffn_norm_g[l]), sh_f, sc_f)
        u = causal_dwconv(h @ w_up[l], ffn_dw_w[l], ffn_dw_b[l])
        gate, val = jnp.split(u, 2, axis=-1)
        x = x + g_f[:, None, :] * ((jax.nn.silu(gate) * val) @ w_down[l])

    return rms_norm(x, final_norm_g)
```

```python
import functools

import jax
import jax.numpy as jnp
from jax import lax
from jax.experimental import pallas as pl
from jax.experimental.pallas import tpu as pltpu

F32 = jnp.float32
BF16 = jnp.bfloat16

V7X_VMEM_BYTES = 64 * 1024 * 1024
V7X_LANES = 128
V7X_SUBLANES = 8

HEAD_DIM = 128
N_HEADS = 8
D_ATTN = N_HEADS * HEAD_DIM
D_CONF = 512
D_SCONV = 512
CONF_WIDTH = 31
SCONV_WIDTH = 3
FFN_WIDTH = 3
N_ADA = 6
RMS_EPS = 1e-6
LN_EPS = 1e-5
CONF_HALO = 32
SCONV_HALO = 16


def _params(semantics, vmem_bytes):
    return pltpu.CompilerParams(dimension_semantics=semantics,
                                vmem_limit_bytes=min(vmem_bytes, V7X_VMEM_BYTES - (6 << 20)))


def _ada_kernel(c_ref, w_ref, b_ref, o_ref):
    c = c_ref[...]
    ca = (c * jax.nn.sigmoid(c)).astype(BF16)
    w = w_ref[0].astype(BF16)
    o_ref[0] = jnp.dot(ca, w, preferred_element_type=F32) + b_ref[0]


def _ada_call(c_pad, ada_w, ada_b):
    L, D, N = ada_w.shape
    tn = 1024
    return pl.pallas_call(
        _ada_kernel,
        grid=(L, N // tn),
        in_specs=[pl.BlockSpec((V7X_SUBLANES, D), lambda l, j: (0, 0)),
                  pl.BlockSpec((1, D, tn), lambda l, j: (l, 0, j)),
                  pl.BlockSpec((1, 1, tn), lambda l, j: (l, 0, j))],
        out_specs=pl.BlockSpec((1, V7X_SUBLANES, tn), lambda l, j: (l, 0, j)),
        out_shape=jax.ShapeDtypeStruct((L, V7X_SUBLANES, N), F32),
        compiler_params=_params(("parallel", "arbitrary"), 32 << 20),
        name="ada",
    )(c_pad, ada_w, ada_b.reshape(L, 1, N))


def _norm_mod_rows(x_ref, g_ref, sh_ref, sc_ref, h_scr, tm, rows=256):
    gs = g_ref[0] * (1.0 + sc_ref[0])
    sh = sh_ref[0]

    def body(r, carry):
        sl = pl.ds(pl.multiple_of(r * rows, rows), rows)
        x = x_ref[0, sl, :]
        ms = jnp.mean(x * x, axis=-1, keepdims=True)
        y = x * lax.rsqrt(ms + RMS_EPS)
        h_scr[sl, :] = (y * gs + sh).astype(BF16)
        return carry

    lax.fori_loop(0, tm // rows, body, 0)


def _mix_in_kernel(x_ref, sh_ref, sc_ref, g_ref, w_ref, wf_ref, bf_ref,
                   proj_ref, logf_ref, h_scr, *, tm, tn, q_tiles, q_scale):
    j = pl.program_id(2)

    @pl.when(j == 0)
    def _():
        _norm_mod_rows(x_ref, g_ref, sh_ref, sc_ref, h_scr, tm)
        z = jnp.dot(h_scr[...], wf_ref[0], preferred_element_type=F32) + bf_ref[0]
        logf_ref[0] = jnp.minimum(z, 0.0) - jnp.log1p(jnp.exp(-jnp.abs(z)))

    r = jnp.dot(h_scr[...], w_ref[0], preferred_element_type=F32)
    r = r * jnp.where(j < q_tiles, q_scale, 1.0)
    for k in range(tn // V7X_LANES):
        proj_ref[0, k] = r[:, k * V7X_LANES:(k + 1) * V7X_LANES].astype(BF16)


def _mix_in_call(x, sh, sc, g, w_main, w_f, b_f, l, *, tm=512, tn=512):
    B, S, D = x.shape
    N = w_main.shape[-1]
    nchunk = N // V7X_LANES
    kern = functools.partial(_mix_in_kernel, tm=tm, tn=tn, q_tiles=D_ATTN // tn,
                             q_scale=HEAD_DIM ** -0.5)
    vmem = 2 * tm * D * 4 + tm * D * 2 + 2 * D * tn * 2 + 2 * D * 128 * 2 \
        + 2 * tm * tn * 2 + 2 * tm * 128 * 4 + 4 * tm * tn * 4 + (8 << 20)
    return pl.pallas_call(
        kern,
        grid=(B, S // tm, N // tn),
        in_specs=[pl.BlockSpec((1, tm, D), lambda b, i, j: (b, i, 0)),
                  pl.BlockSpec((1, 1, D), lambda b, i, j: (b, 0, 0)),
                  pl.BlockSpec((1, 1, D), lambda b, i, j: (b, 0, 0)),
                  pl.BlockSpec((1, 1, D), lambda b, i, j: (l, 0, 0)),
                  pl.BlockSpec((1, D, tn), lambda b, i, j: (l, 0, j)),
                  pl.BlockSpec((1, D, V7X_LANES), lambda b, i, j: (l, 0, 0)),
                  pl.BlockSpec((1, 1, V7X_LANES), lambda b, i, j: (l, 0, 0))],
        out_specs=[pl.BlockSpec((1, tn // V7X_LANES, tm, V7X_LANES), lambda b, i, j: (b, j, i, 0)),
                   pl.BlockSpec((1, tm, V7X_LANES), lambda b, i, j: (b, i, 0))],
        out_shape=[jax.ShapeDtypeStruct((B, nchunk, S, V7X_LANES), BF16),
                   jax.ShapeDtypeStruct((B, S, V7X_LANES), F32)],
        scratch_shapes=[pltpu.VMEM((tm, D), BF16)],
        compiler_params=_params(("parallel", "parallel", "arbitrary"), vmem),
        name="mix_in",
    )(x, sh, sc, g, w_main, w_f, b_f)


def _cumsum_kernel(l_ref, o_ref, carry_ref, *, ts):
    i = pl.program_id(1)

    @pl.when(i == 0)
    def _():
        carry_ref[...] = jnp.zeros_like(carry_ref)

    lf = l_ref[0]
    row = lax.broadcasted_iota(jnp.int32, (ts, ts), 0)
    col = lax.broadcasted_iota(jnp.int32, (ts, ts), 1)
    tri = jnp.where(col <= row, 1.0, 0.0).astype(BF16)
    hi = lf.astype(BF16)
    r1 = lf - hi.astype(F32)
    mid = r1.astype(BF16)
    lo = (r1 - mid.astype(F32)).astype(BF16)
    f = (jnp.dot(tri, hi, preferred_element_type=F32)
         + jnp.dot(tri, mid, preferred_element_type=F32)
         + jnp.dot(tri, lo, preferred_element_type=F32)
         + carry_ref[0:1, :])
    carry_ref[...] = jnp.broadcast_to(f[ts - 1:ts, :], carry_ref.shape)
    o_ref[0] = (-f).T[:N_HEADS, :]


def _cumsum_call(logf, *, ts=512):
    B, S, _ = logf.shape
    return pl.pallas_call(
        functools.partial(_cumsum_kernel, ts=ts),
        grid=(B, S // ts),
        in_specs=[pl.BlockSpec((1, ts, V7X_LANES), lambda b, i: (b, i, 0))],
        out_specs=pl.BlockSpec((1, N_HEADS, ts), lambda b, i: (b, 0, i)),
        out_shape=jax.ShapeDtypeStruct((B, N_HEADS, S), F32),
        scratch_shapes=[pltpu.VMEM((V7X_SUBLANES, V7X_LANES), F32)],
        compiler_params=_params(("parallel", "arbitrary"), 32 << 20),
        name="forget_cumsum",
    )(logf)


def _attn_kernel(q_ref, k_ref, v_ref, nf_ref, o_ref, *, tq, tk):
    i = pl.program_id(2)
    q = q_ref[0, 0]

    def step(start, carry, masked):
        m, l, acc = carry
        kt = k_ref[0, 0, pl.ds(start, tk), :]
        vt = v_ref[0, 0, pl.ds(start, tk), :]
        s = lax.dot_general(q, kt, (((1,), (1,)), ((), ())), preferred_element_type=F32)
        s = s + nf_ref[0, 0, :, pl.ds(start, tk)]
        if masked:
            row = lax.broadcasted_iota(jnp.int32, (tq, tk), 0)
            col = lax.broadcasted_iota(jnp.int32, (tq, tk), 1)
            s = jnp.where(col <= row, s, -jnp.inf)
        m_new = jnp.maximum(m, jnp.max(s, axis=-1, keepdims=True))
        p = jnp.exp(s - m_new)
        alpha = jnp.exp(m - m_new)
        l = alpha * l + jnp.sum(p, axis=-1, keepdims=True)
        acc = alpha * acc + jnp.dot(p.astype(BF16), vt, preferred_element_type=F32)
        return m_new, l, acc

    init = (jnp.full((tq, 1), -jnp.inf, F32), jnp.zeros((tq, 1), F32),
            jnp.zeros((tq, HEAD_DIM), F32))
    carry = lax.fori_loop(
        0, i, lambda j, c: step(pl.multiple_of(j * tk, tk), c, False), init)
    m, l, acc = step(pl.multiple_of(i * tk, tk), carry, True)
    o_ref[0] = (acc / l).astype(BF16)


def _attn_call(proj, nf, *, tq=512):
    B, _, S, _ = proj.shape
    tk = tq
    nf4 = nf.reshape(B, N_HEADS, 1, S)
    return pl.pallas_call(
        functools.partial(_attn_kernel, tq=tq, tk=tk),
        grid=(B, N_HEADS, S // tq),
        in_specs=[pl.BlockSpec((1, 1, tq, HEAD_DIM), lambda b, h, i: (b, h, i, 0)),
                  pl.BlockSpec((1, 1, S, HEAD_DIM), lambda b, h, i: (b, N_HEADS + h, 0, 0)),
                  pl.BlockSpec((1, 1, S, HEAD_DIM), lambda b, h, i: (b, 2 * N_HEADS + h, 0, 0)),
                  pl.BlockSpec((1, 1, 1, S), lambda b, h, i: (b, h, 0, 0))],
        out_specs=pl.BlockSpec((1, tq, HEAD_DIM), lambda b, h, i: (b, i, h)),
        out_shape=jax.ShapeDtypeStruct((B, S, D_ATTN), BF16),
        compiler_params=_params(("parallel", "parallel", "arbitrary"), 40 << 20),
        name="fox_attn",
    )(proj, proj, proj, nf4)


def _conv_kernel(cv_ref, cg_ref, cvh_ref, cgh_ref, sx_ref, sb_ref, sc_ref, sxh_ref, sch_ref,
                 cw_ref, cb_ref, lg_ref, lb_ref, sw_ref, o_ref, cbuf, sbuf, *, ts, rows):
    first = pl.program_id(1) == 0
    nl = D_CONF // V7X_LANES
    for c in range(nl):
        cols = slice(c * V7X_LANES, (c + 1) * V7X_LANES)
        gh = cvh_ref[0, c].astype(F32) * jax.nn.sigmoid(cgh_ref[0, c].astype(F32))
        cbuf[0:CONF_HALO, cols] = jnp.where(first, 0.0, gh)
        cbuf[CONF_HALO:, cols] = cv_ref[0, c].astype(F32) * jax.nn.sigmoid(cg_ref[0, c].astype(F32))
        zh = sch_ref[0, c].astype(F32) * sxh_ref[0, c].astype(F32)
        sbuf[0:SCONV_HALO, cols] = jnp.where(first, 0.0, zh)
        sbuf[SCONV_HALO:, cols] = sc_ref[0, c].astype(F32) * sx_ref[0, c].astype(F32)

    cb = cb_ref[0]
    lg = lg_ref[0]
    lb = lb_ref[0]
    for r in range(ts // rows):
        r0 = r * rows
        acc = jnp.zeros((rows, D_CONF), F32)
        for k in range(CONF_WIDTH):
            o = r0 + CONF_HALO - (CONF_WIDTH - 1) + k
            acc = acc + cw_ref[0, k:k + 1, :] * cbuf[o:o + rows, :]
        acc = acc + cb
        mu = jnp.mean(acc, axis=-1, keepdims=True)
        xc = acc - mu
        var = jnp.mean(xc * xc, axis=-1, keepdims=True)
        y = xc * lax.rsqrt(var + LN_EPS) * lg + lb
        o_ref[0, r0:r0 + rows, 0:D_CONF] = (y * jax.nn.sigmoid(y)).astype(BF16)

        sacc = jnp.zeros((rows, D_SCONV), F32)
        for k in range(SCONV_WIDTH):
            o = r0 + SCONV_HALO - (SCONV_WIDTH - 1) + k
            sacc = sacc + sw_ref[0, k:k + 1, :] * sbuf[o:o + rows, :]
        for c in range(nl):
            cols = slice(c * V7X_LANES, (c + 1) * V7X_LANES)
            gated = sb_ref[0, c, r0:r0 + rows, :].astype(F32) * sacc[:, cols]
            o_ref[0, r0:r0 + rows, D_CONF + c * V7X_LANES:D_CONF + (c + 1) * V7X_LANES] = gated.astype(BF16)


def _conv_call(proj, conf_dw_w, conf_dw_b, conf_ln_g, conf_ln_b, sc_dw_w, l, *, ts=256, rows=32):
    B, _, S, _ = proj.shape
    L = conf_dw_w.shape[0]
    nl = D_CONF // V7X_LANES
    base = 3 * D_ATTN // V7X_LANES // nl

    def tile(g):
        return pl.BlockSpec((1, nl, ts, V7X_LANES), lambda b, i: (b, base + g, i, 0))

    def halo(g, n):
        return pl.BlockSpec((1, nl, n, V7X_LANES),
                            lambda b, i: (b, base + g, jnp.maximum(i * (ts // n) - 1, 0), 0))

    def wspec(k):
        return pl.BlockSpec((1, k, D_CONF), lambda b, i: (l, 0, 0))

    return pl.pallas_call(
        functools.partial(_conv_kernel, ts=ts, rows=rows),
        grid=(B, S // ts),
        in_specs=[tile(0), tile(1), halo(0, CONF_HALO), halo(1, CONF_HALO),
                  tile(2), tile(3), tile(4), halo(2, SCONV_HALO), halo(4, SCONV_HALO),
                  wspec(CONF_WIDTH), wspec(1), wspec(1), wspec(1), wspec(SCONV_WIDTH)],
        out_specs=pl.BlockSpec((1, ts, D_CONF + D_SCONV), lambda b, i: (b, i, 0)),
        out_shape=jax.ShapeDtypeStruct((B, S, D_CONF + D_SCONV), BF16),
        scratch_shapes=[pltpu.VMEM((ts + CONF_HALO, D_CONF), F32),
                        pltpu.VMEM((ts + SCONV_HALO, D_SCONV), F32)],
        compiler_params=_params(("parallel", "arbitrary"), 32 << 20),
        name="conv_mixers",
    )(proj, proj, proj, proj, proj, proj, proj, proj, proj,
      conf_dw_w, conf_dw_b.reshape(L, 1, D_CONF), conf_ln_g.reshape(L, 1, D_CONF),
      conf_ln_b.reshape(L, 1, D_CONF), sc_dw_w)


def _mix_out_kernel(a_ref, c_ref, w_ref, x_ref, g_ref, o_ref):
    ka = a_ref.shape[-1]
    r = jnp.dot(a_ref[0], w_ref[0, 0:ka, :], preferred_element_type=F32)
    r = r + jnp.dot(c_ref[0], w_ref[0, ka:, :], preferred_element_type=F32)
    o_ref[0] = x_ref[0] + g_ref[0] * r


def _mix_out_call(attn, conv, w_out, x, gate, l, *, tm=512):
    B, S, D = x.shape
    ka, kc = attn.shape[-1], conv.shape[-1]
    vmem = 2 * (ka + kc) * D * 2 + 2 * tm * (ka + kc) * 2 + 4 * tm * D * 4 + 2 * tm * D * 4 + (8 << 20)
    return pl.pallas_call(
        _mix_out_kernel,
        grid=(B, S // tm),
        in_specs=[pl.BlockSpec((1, tm, ka), lambda b, i: (b, i, 0)),
                  pl.BlockSpec((1, tm, kc), lambda b, i: (b, i, 0)),
                  pl.BlockSpec((1, ka + kc, D), lambda b, i: (l, 0, 0)),
                  pl.BlockSpec((1, tm, D), lambda b, i: (b, i, 0)),
                  pl.BlockSpec((1, 1, D), lambda b, i: (b, 0, 0))],
        out_specs=pl.BlockSpec((1, tm, D), lambda b, i: (b, i, 0)),
        out_shape=jax.ShapeDtypeStruct((B, S, D), F32),
        compiler_params=_params(("parallel", "parallel"), vmem),
        name="mix_out",
    )(attn, conv, w_out, x, gate)


def _ffn_kernel(x_ref, sh_ref, sc_ref, gt_ref, g_ref, wg_ref, wv_ref, dwg_ref, dwv_ref,
                dbg_ref, dbv_ref, wd_ref, o_ref, h_scr, acc_scr, ubuf, carry_ref, *, tm, tf):
    i = pl.program_id(1)
    j = pl.program_id(2)
    pad = V7X_SUBLANES

    @pl.when(j == 0)
    def _():
        _norm_mod_rows(x_ref, g_ref, sh_ref, sc_ref, h_scr, tm)
        acc_scr[...] = jnp.zeros_like(acc_scr)

    h = h_scr[...]

    def conv_branch(idx, w_ref, dw_ref, db_ref):
        u = jnp.dot(h, w_ref[0], preferred_element_type=F32)
        prev = carry_ref[j, idx]
        ubuf[idx, 0:pad, :] = jnp.where(i == 0, 0.0, prev)
        ubuf[idx, pad:, :] = u
        carry_ref[j, idx] = u[tm - pad:, :]
        return (dw_ref[0, 2:3, :] * u
                + dw_ref[0, 1:2, :] * ubuf[idx, pad - 1:pad - 1 + tm, :]
                + dw_ref[0, 0:1, :] * ubuf[idx, pad - 2:pad - 2 + tm, :]
                + db_ref[0])

    yg = conv_branch(0, wg_ref, dwg_ref, dbg_ref)
    yv = conv_branch(1, wv_ref, dwv_ref, dbv_ref)
    act = (yg * jax.nn.sigmoid(yg) * yv).astype(BF16)
    acc_scr[...] += jnp.dot(act, wd_ref[0], preferred_element_type=F32)

    @pl.when(j == pl.num_programs(2) - 1)
    def _():
        o_ref[0] = x_ref[0] + gt_ref[0] * acc_scr[...]


def _ffn_call(x, sh, sc, gate, g, w_up, dw_w, dw_b, w_down, l, *, tm=512, tf=512):
    B, S, D = x.shape
    L, FF, _ = w_down.shape
    nf = FF // tf
    dw_b3 = dw_b.reshape(L, 1, 2 * FF)
    vmem = 4 * tm * D * 4 + tm * D * 2 + tm * D * 4 + 2 * (tm + 8) * tf * 4 \
        + 2 * 2 * D * tf * 2 + 2 * tf * D * 2 + 8 * tm * tf * 4 + (8 << 20)
    return pl.pallas_call(
        functools.partial(_ffn_kernel, tm=tm, tf=tf),
        grid=(B, S // tm, nf),
        in_specs=[pl.BlockSpec((1, tm, D), lambda b, i, j: (b, i, 0)),
                  pl.BlockSpec((1, 1, D), lambda b, i, j: (b, 0, 0)),
                  pl.BlockSpec((1, 1, D), lambda b, i, j: (b, 0, 0)),
                  pl.BlockSpec((1, 1, D), lambda b, i, j: (b, 0, 0)),
                  pl.BlockSpec((1, 1, D), lambda b, i, j: (l, 0, 0)),
                  pl.BlockSpec((1, D, tf), lambda b, i, j: (l, 0, j)),
                  pl.BlockSpec((1, D, tf), lambda b, i, j: (l, 0, nf + j)),
                  pl.BlockSpec((1, FFN_WIDTH, tf), lambda b, i, j: (l, 0, j)),
                  pl.BlockSpec((1, FFN_WIDTH, tf), lambda b, i, j: (l, 0, nf + j)),
                  pl.BlockSpec((1, 1, tf), lambda b, i, j: (l, 0, j)),
                  pl.BlockSpec((1, 1, tf), lambda b, i, j: (l, 0, nf + j)),
                  pl.BlockSpec((1, tf, D), lambda b, i, j: (l, j, 0))],
        out_specs=pl.BlockSpec((1, tm, D), lambda b, i, j: (b, i, 0)),
        out_shape=jax.ShapeDtypeStruct((B, S, D), F32),
        scratch_shapes=[pltpu.VMEM((tm, D), BF16),
                        pltpu.VMEM((tm, D), F32),
                        pltpu.VMEM((2, tm + V7X_SUBLANES, tf), F32),
                        pltpu.VMEM((nf, 2, V7X_SUBLANES, tf), F32)],
        compiler_params=_params(("parallel", "arbitrary", "arbitrary"), vmem),
        name="conv_ffn",
    )(x, sh, sc, gate, g, w_up, w_up, dw_w, dw_w, dw_b3, dw_b3, w_down)


def _final_norm_kernel(x_ref, g_ref, o_ref):
    x = x_ref[0]
    ms = jnp.mean(x * x, axis=-1, keepdims=True)
    o_ref[0] = x * lax.rsqrt(ms + RMS_EPS) * g_ref[...]


def _final_norm_call(x, g, *, tm=256):
    B, S, D = x.shape
    return pl.pallas_call(
        _final_norm_kernel,
        grid=(B, S // tm),
        in_specs=[pl.BlockSpec((1, tm, D), lambda b, i: (b, i, 0)),
                  pl.BlockSpec((1, D), lambda b, i: (0, 0))],
        out_specs=pl.BlockSpec((1, tm, D), lambda b, i: (b, i, 0)),
        out_shape=jax.ShapeDtypeStruct((B, S, D), F32),
        compiler_params=_params(("parallel", "parallel"), 32 << 20),
        name="final_norm",
    )(x, g.reshape(1, D))


def kernel(x, c, ada_w, ada_b, mix_norm_g, w_in, b_forget, conf_dw_w, conf_dw_b, conf_ln_g,
           conf_ln_b, sc_dw_w, w_out, ffn_norm_g, w_up, ffn_dw_w, ffn_dw_b, w_down, final_norm_g):
    B, S, D = x.shape
    L = ada_w.shape[0]

    f0 = 3 * D_ATTN
    w_main = jnp.concatenate([w_in[..., :f0], w_in[..., f0 + N_HEADS:]], axis=-1).astype(BF16)
    w_f = jnp.pad(w_in[..., f0:f0 + N_HEADS], ((0, 0), (0, 0), (0, V7X_LANES - N_HEADS))).astype(BF16)
    b_f = jnp.pad(b_forget, ((0, 0), (0, V7X_LANES - N_HEADS))).reshape(L, 1, V7X_LANES)
    w_out_b = w_out.astype(BF16)
    w_up_b = w_up.astype(BF16)
    w_down_b = w_down.astype(BF16)

    mix_g = mix_norm_g.reshape(L, 1, D)
    ffn_g = ffn_norm_g.reshape(L, 1, D)
    c_pad = jnp.pad(c, ((0, V7X_SUBLANES - B), (0, 0)))
    ada = _ada_call(c_pad, ada_w, ada_b)[:, :B]
    ada = ada.reshape(L, B, N_ADA, 1, D)

    for l in range(L):
        sh_m, sc_m, g_m, sh_f, sc_f, g_f = (ada[l, :, k] for k in range(N_ADA))
        proj, logf = _mix_in_call(x, sh_m, sc_m, mix_g, w_main, w_f, b_f, l)
        nf = _cumsum_call(logf)
        attn = _attn_call(proj, nf)
        conv = _conv_call(proj, conf_dw_w, conf_dw_b, conf_ln_g, conf_ln_b, sc_dw_w, l)
        x = _mix_out_call(attn, conv, w_out_b, x, g_m, l)
        x = _ffn_call(x, sh_f, sc_f, g_f, ffn_g, w_up_b, ffn_dw_w, ffn_dw_b, w_down_b, l)
    return _final_norm_call(x, final_norm_g)
```

```python
import functools

import jax
import jax.numpy as jnp
from jax import lax
from jax.experimental import pallas as pl
from jax.experimental.pallas import tpu as pltpu

F32 = jnp.float32
BF16 = jnp.bfloat16

V7X_VMEM_BYTES = 64 * 1024 * 1024
V7X_LANES = 128
V7X_SUBLANES = 8

HEAD_DIM = 128
N_HEADS = 8
D_ATTN = N_HEADS * HEAD_DIM
D_CONF = 512
D_SCONV = 512
CONF_WIDTH = 31
SCONV_WIDTH = 3
FFN_WIDTH = 3
N_ADA = 6
RMS_EPS = 1e-6
LN_EPS = 1e-5
CONF_HALO = 32
SCONV_HALO = 16
LOG2E = 1.4426950408889634


def _params(semantics, vmem_bytes):
    return pltpu.CompilerParams(dimension_semantics=semantics,
                                vmem_limit_bytes=min(vmem_bytes, V7X_VMEM_BYTES - (6 << 20)))


def _ada_kernel(c_ref, w_ref, b_ref, o_ref):
    c = c_ref[...]
    ca = (c * jax.nn.sigmoid(c)).astype(BF16)
    w = w_ref[0].astype(BF16)
    o_ref[0] = jnp.dot(ca, w, preferred_element_type=F32) + b_ref[0]


def _ada_call(c_pad, ada_w, ada_b):
    L, D, N = ada_w.shape
    tn = 1024
    return pl.pallas_call(
        _ada_kernel,
        grid=(L, N // tn),
        in_specs=[pl.BlockSpec((V7X_SUBLANES, D), lambda l, j: (0, 0)),
                  pl.BlockSpec((1, D, tn), lambda l, j: (l, 0, j)),
                  pl.BlockSpec((1, 1, tn), lambda l, j: (l, 0, j))],
        out_specs=pl.BlockSpec((1, V7X_SUBLANES, tn), lambda l, j: (l, 0, j)),
        out_shape=jax.ShapeDtypeStruct((L, V7X_SUBLANES, N), F32),
        compiler_params=_params(("parallel", "arbitrary"), 32 << 20),
        name="ada",
    )(c_pad, ada_w, ada_b.reshape(L, 1, N))


def _norm_mod_rows(x_ref, g_ref, sh_ref, sc_ref, h_scr, tm, rows=256):
    gs = g_ref[0] * (1.0 + sc_ref[0])
    sh = sh_ref[0]

    def body(r, carry):
        sl = pl.ds(pl.multiple_of(r * rows, rows), rows)
        x = x_ref[0, sl, :]
        ms = jnp.mean(x * x, axis=-1, keepdims=True)
        y = x * lax.rsqrt(ms + RMS_EPS)
        h_scr[sl, :] = (y * gs + sh).astype(BF16)
        return carry

    lax.fori_loop(0, tm // rows, body, 0)


def _mix_in_kernel(x_ref, sh_ref, sc_ref, g_ref, w_ref, cs_ref, wf_ref, bf_ref,
                   proj_ref, logf_ref, h_scr, *, tm, tn):
    _norm_mod_rows(x_ref, g_ref, sh_ref, sc_ref, h_scr, tm)
    h = h_scr[...]
    z = jnp.dot(h, wf_ref[0], preferred_element_type=F32) + bf_ref[0]
    logf_ref[0] = jnp.minimum(z, 0.0) - jnp.log1p(jnp.exp(-jnp.abs(z)))
    per = tn // V7X_LANES
    for c in range(w_ref.shape[-1] // tn):
        cols = slice(c * tn, (c + 1) * tn)
        r = jnp.dot(h, w_ref[0, :, cols], preferred_element_type=F32) * cs_ref[:, cols]
        for k in range(per):
            proj_ref[0, c * per + k] = r[:, k * V7X_LANES:(k + 1) * V7X_LANES].astype(BF16)


def _mix_in_call(x, sh, sc, g, w_main, col_scale, w_f, b_f, l, *, tm=512, tn=512):
    B, S, D = x.shape
    N = w_main.shape[-1]
    nchunk = N // V7X_LANES
    vmem = 2 * tm * D * 4 + tm * D * 2 + D * N * 2 + 2 * D * V7X_LANES * 2 \
        + 2 * tm * N * 2 + 2 * tm * V7X_LANES * 4 + 4 * tm * tn * 4 + (4 << 20)
    return pl.pallas_call(
        functools.partial(_mix_in_kernel, tm=tm, tn=tn),
        grid=(B, S // tm),
        in_specs=[pl.BlockSpec((1, tm, D), lambda b, i: (b, i, 0)),
                  pl.BlockSpec((1, 1, D), lambda b, i: (b, 0, 0)),
                  pl.BlockSpec((1, 1, D), lambda b, i: (b, 0, 0)),
                  pl.BlockSpec((1, 1, D), lambda b, i: (l, 0, 0)),
                  pl.BlockSpec((1, D, N), lambda b, i: (l, 0, 0), pipeline_mode=pl.Buffered(1)),
                  pl.BlockSpec((1, N), lambda b, i: (0, 0)),
                  pl.BlockSpec((1, D, V7X_LANES), lambda b, i: (l, 0, 0)),
                  pl.BlockSpec((1, 1, V7X_LANES), lambda b, i: (l, 0, 0))],
        out_specs=[pl.BlockSpec((1, nchunk, tm, V7X_LANES), lambda b, i: (b, 0, i, 0)),
                   pl.BlockSpec((1, tm, V7X_LANES), lambda b, i: (b, i, 0))],
        out_shape=[jax.ShapeDtypeStruct((B, nchunk, S, V7X_LANES), BF16),
                   jax.ShapeDtypeStruct((B, S, V7X_LANES), F32)],
        scratch_shapes=[pltpu.VMEM((tm, D), BF16)],
        compiler_params=_params(("parallel", "parallel"), vmem),
        name="mix_in",
    )(x, sh, sc, g, w_main, col_scale, w_f, b_f)


def _cumsum_kernel(l_ref, o_ref, carry_ref, *, ts):
    i = pl.program_id(1)

    @pl.when(i == 0)
    def _():
        carry_ref[...] = jnp.zeros_like(carry_ref)

    lf = l_ref[0]
    row = lax.broadcasted_iota(jnp.int32, (ts, ts), 0)
    col = lax.broadcasted_iota(jnp.int32, (ts, ts), 1)
    tri = jnp.where(col <= row, 1.0, 0.0).astype(BF16)
    hi = lf.astype(BF16)
    r1 = lf - hi.astype(F32)
    mid = r1.astype(BF16)
    lo = (r1 - mid.astype(F32)).astype(BF16)
    f = (jnp.dot(tri, hi, preferred_element_type=F32)
         + jnp.dot(tri, mid, preferred_element_type=F32)
         + jnp.dot(tri, lo, preferred_element_type=F32)
         + carry_ref[0:1, :])
    carry_ref[...] = jnp.broadcast_to(f[ts - 1:ts, :], carry_ref.shape)
    o_ref[0] = (f * (-LOG2E)).T[:N_HEADS, :]


def _cumsum_call(logf, *, ts=512):
    B, S, _ = logf.shape
    return pl.pallas_call(
        functools.partial(_cumsum_kernel, ts=ts),
        grid=(B, S // ts),
        in_specs=[pl.BlockSpec((1, ts, V7X_LANES), lambda b, i: (b, i, 0))],
        out_specs=pl.BlockSpec((1, N_HEADS, ts), lambda b, i: (b, 0, i)),
        out_shape=jax.ShapeDtypeStruct((B, N_HEADS, S), F32),
        scratch_shapes=[pltpu.VMEM((V7X_SUBLANES, V7X_LANES), F32)],
        compiler_params=_params(("parallel", "arbitrary"), 32 << 20),
        name="forget_cumsum",
    )(logf)


def _attn_kernel(q_ref, k_ref, v_ref, nf_ref, o_ref, *, tk, nsub):
    i = pl.program_id(2)

    def tile(a, start, carry, masked):
        m, l, acc = carry
        q = q_ref[0, 0, a * tk:(a + 1) * tk, :]
        kt = k_ref[0, 0, pl.ds(start, tk), :]
        vt = v_ref[0, 0, pl.ds(start, tk), :]
        s = lax.dot_general(q, kt, (((1,), (1,)), ((), ())), preferred_element_type=F32)
        s = s + nf_ref[0, 0, :, pl.ds(start, tk)]
        if masked:
            row = lax.broadcasted_iota(jnp.int32, (tk, tk), 0)
            col = lax.broadcasted_iota(jnp.int32, (tk, tk), 1)
            s = jnp.where(col <= row, s, -jnp.inf)
        m_new = jnp.maximum(m, jnp.max(s, axis=-1, keepdims=True))
        p = jnp.exp2(s - m_new)
        alpha = jnp.exp2(m - m_new)
        l = alpha * l + jnp.sum(p, axis=-1, keepdims=True)
        acc = alpha * acc + jnp.dot(p.astype(BF16), vt, preferred_element_type=F32)
        return m_new, l, acc

    def body(j, carries):
        start = pl.multiple_of(j * tk, tk)
        return tuple(tile(a, start, carries[a], False) for a in range(nsub))

    init = (jnp.full((tk, 1), -jnp.inf, F32), jnp.zeros((tk, 1), F32),
            jnp.zeros((tk, HEAD_DIM), F32))
    carries = list(lax.fori_loop(0, i * nsub, body, (init,) * nsub))
    for t in range(nsub):
        start = pl.multiple_of((i * nsub + t) * tk, tk)
        for a in range(t, nsub):
            carries[a] = tile(a, start, carries[a], a == t)
    for a in range(nsub):
        _, l, acc = carries[a]
        o_ref[0, a * tk:(a + 1) * tk, :] = (acc / l).astype(BF16)


def _attn_call(proj, nf, *, tk=512, nsub=2):
    B, _, S, _ = proj.shape
    tq = tk * nsub
    nf4 = nf.reshape(B, N_HEADS, 1, S)
    return pl.pallas_call(
        functools.partial(_attn_kernel, tk=tk, nsub=nsub),
        grid=(B, N_HEADS, S // tq),
        in_specs=[pl.BlockSpec((1, 1, tq, HEAD_DIM), lambda b, h, i: (b, h, i, 0)),
                  pl.BlockSpec((1, 1, S, HEAD_DIM), lambda b, h, i: (b, N_HEADS + h, 0, 0)),
                  pl.BlockSpec((1, 1, S, HEAD_DIM), lambda b, h, i: (b, 2 * N_HEADS + h, 0, 0)),
                  pl.BlockSpec((1, 1, 1, S), lambda b, h, i: (b, h, 0, 0))],
        out_specs=pl.BlockSpec((1, tq, HEAD_DIM), lambda b, h, i: (b, i, h)),
        out_shape=jax.ShapeDtypeStruct((B, S, D_ATTN), BF16),
        compiler_params=_params(("parallel", "parallel", "arbitrary"), 40 << 20),
        name="fox_attn",
    )(proj, proj, proj, nf4)


def _conv_kernel(cv_ref, cg_ref, cvh_ref, cgh_ref, sx_ref, sb_ref, sc_ref, sxh_ref, sch_ref,
                 cw_ref, cb_ref, lg_ref, lb_ref, sw_ref, o_ref, cbuf, sbuf, *, ts, rows):
    first = pl.program_id(1) == 0
    nl = D_CONF // V7X_LANES
    for c in range(nl):
        cols = slice(c * V7X_LANES, (c + 1) * V7X_LANES)
        gh = cvh_ref[0, c].astype(F32) * jax.nn.sigmoid(cgh_ref[0, c].astype(F32))
        cbuf[0:CONF_HALO, cols] = jnp.where(first, 0.0, gh)
        cbuf[CONF_HALO:, cols] = cv_ref[0, c].astype(F32) * jax.nn.sigmoid(cg_ref[0, c].astype(F32))
        zh = sch_ref[0, c].astype(F32) * sxh_ref[0, c].astype(F32)
        sbuf[0:SCONV_HALO, cols] = jnp.where(first, 0.0, zh)
        sbuf[SCONV_HALO:, cols] = sc_ref[0, c].astype(F32) * sx_ref[0, c].astype(F32)

    cb = cb_ref[0]
    lg = lg_ref[0]
    lb = lb_ref[0]
    for r in range(ts // rows):
        r0 = r * rows
        acc = jnp.zeros((rows, D_CONF), F32)
        for k in range(CONF_WIDTH):
            o = r0 + CONF_HALO - (CONF_WIDTH - 1) + k
            acc = acc + cw_ref[0, k:k + 1, :] * cbuf[o:o + rows, :]
        acc = acc + cb
        mu = jnp.mean(acc, axis=-1, keepdims=True)
        xc = acc - mu
        var = jnp.mean(xc * xc, axis=-1, keepdims=True)
        y = xc * lax.rsqrt(var + LN_EPS) * lg + lb
        o_ref[0, r0:r0 + rows, 0:D_CONF] = (y * jax.nn.sigmoid(y)).astype(BF16)

        sacc = jnp.zeros((rows, D_SCONV), F32)
        for k in range(SCONV_WIDTH):
            o = r0 + SCONV_HALO - (SCONV_WIDTH - 1) + k
            sacc = sacc + sw_ref[0, k:k + 1, :] * sbuf[o:o + rows, :]
        for c in range(nl):
            cols = slice(c * V7X_LANES, (c + 1) * V7X_LANES)
            gated = sb_ref[0, c, r0:r0 + rows, :].astype(F32) * sacc[:, cols]
            o_ref[0, r0:r0 + rows, D_CONF + c * V7X_LANES:D_CONF + (c + 1) * V7X_LANES] = gated.astype(BF16)


def _conv_call(proj, conf_dw_w, conf_dw_b, conf_ln_g, conf_ln_b, sc_dw_w, l, *, ts=256, rows=32):
    B, _, S, _ = proj.shape
    L = conf_dw_w.shape[0]
    nl = D_CONF // V7X_LANES
    base = 3 * D_ATTN // V7X_LANES // nl

    def tile(g):
        return pl.BlockSpec((1, nl, ts, V7X_LANES), lambda b, i: (b, base + g, i, 0))

    def halo(g, n):
        return pl.BlockSpec((1, nl, n, V7X_LANES),
                            lambda b, i: (b, base + g, jnp.maximum(i * (ts // n) - 1, 0), 0))

    def wspec(k):
        return pl.BlockSpec((1, k, D_CONF), lambda b, i: (l, 0, 0))

    return pl.pallas_call(
        functools.partial(_conv_kernel, ts=ts, rows=rows),
        grid=(B, S // ts),
        in_specs=[tile(0), tile(1), halo(0, CONF_HALO), halo(1, CONF_HALO),
                  tile(2), tile(3), tile(4), halo(2, SCONV_HALO), halo(4, SCONV_HALO),
                  wspec(CONF_WIDTH), wspec(1), wspec(1), wspec(1), wspec(SCONV_WIDTH)],
        out_specs=pl.BlockSpec((1, ts, D_CONF + D_SCONV), lambda b, i: (b, i, 0)),
        out_shape=jax.ShapeDtypeStruct((B, S, D_CONF + D_SCONV), BF16),
        scratch_shapes=[pltpu.VMEM((ts + CONF_HALO, D_CONF), F32),
                        pltpu.VMEM((ts + SCONV_HALO, D_SCONV), F32)],
        compiler_params=_params(("parallel", "arbitrary"), 32 << 20),
        name="conv_mixers",
    )(proj, proj, proj, proj, proj, proj, proj, proj, proj,
      conf_dw_w, conf_dw_b.reshape(L, 1, D_CONF), conf_ln_g.reshape(L, 1, D_CONF),
      conf_ln_b.reshape(L, 1, D_CONF), sc_dw_w)


def _mix_out_kernel(a_ref, c_ref, w_ref, x_ref, g_ref, o_ref):
    ka = a_ref.shape[-1]
    r = jnp.dot(a_ref[0], w_ref[0, 0:ka, :], preferred_element_type=F32)
    r = r + jnp.dot(c_ref[0], w_ref[0, ka:, :], preferred_element_type=F32)
    o_ref[0] = x_ref[0] + g_ref[0] * r


def _mix_out_call(attn, conv, w_out, x, gate, l, *, tm=512):
    B, S, D = x.shape
    ka, kc = attn.shape[-1], conv.shape[-1]
    vmem = 2 * (ka + kc) * D * 2 + 2 * tm * (ka + kc) * 2 + 4 * tm * D * 4 + 2 * tm * D * 4 + (8 << 20)
    return pl.pallas_call(
        _mix_out_kernel,
        grid=(B, S // tm),
        in_specs=[pl.BlockSpec((1, tm, ka), lambda b, i: (b, i, 0)),
                  pl.BlockSpec((1, tm, kc), lambda b, i: (b, i, 0)),
                  pl.BlockSpec((1, ka + kc, D), lambda b, i: (l, 0, 0)),
                  pl.BlockSpec((1, tm, D), lambda b, i: (b, i, 0)),
                  pl.BlockSpec((1, 1, D), lambda b, i: (b, 0, 0))],
        out_specs=pl.BlockSpec((1, tm, D), lambda b, i: (b, i, 0)),
        out_shape=jax.ShapeDtypeStruct((B, S, D), F32),
        compiler_params=_params(("parallel", "parallel"), vmem),
        name="mix_out",
    )(attn, conv, w_out, x, gate)


def _ffn_kernel(x_ref, sh_ref, sc_ref, gt_ref, g_ref, wg_ref, wv_ref, dwg_ref, dwv_ref,
                dbg_ref, dbv_ref, wd_ref, o_ref, h_scr, ubuf, carry_ref, *, tm, tf):
    i = pl.program_id(1)
    j = pl.program_id(2)
    pad = V7X_SUBLANES

    @pl.when(j == 0)
    def _():
        _norm_mod_rows(x_ref, g_ref, sh_ref, sc_ref, h_scr, tm)
        o_ref[...] = jnp.zeros_like(o_ref)

    h = h_scr[...]

    def conv_branch(idx, w_ref, dw_ref, db_ref):
        u = jnp.dot(h, w_ref[0], preferred_element_type=F32)
        prev = carry_ref[j, idx]
        ubuf[idx, 0:pad, :] = jnp.where(i == 0, 0.0, prev)
        ubuf[idx, pad:, :] = u
        carry_ref[j, idx] = u[tm - pad:, :]
        return (dw_ref[0, 2:3, :] * u
                + dw_ref[0, 1:2, :] * ubuf[idx, pad - 1:pad - 1 + tm, :]
                + dw_ref[0, 0:1, :] * ubuf[idx, pad - 2:pad - 2 + tm, :]
                + db_ref[0])

    yg = conv_branch(0, wg_ref, dwg_ref, dbg_ref)
    yv = conv_branch(1, wv_ref, dwv_ref, dbv_ref)
    act = (yg * jax.nn.sigmoid(yg) * yv).astype(BF16)
    o_ref[0] += jnp.dot(act, wd_ref[0], preferred_element_type=F32)

    @pl.when(j == pl.num_programs(2) - 1)
    def _():
        o_ref[0] = x_ref[0] + gt_ref[0] * o_ref[0]


def _ffn_call(x, sh, sc, gate, g, w_up, dw_w, dw_b, w_down, l, *, tm=1024, tf=512):
    B, S, D = x.shape
    L, FF, _ = w_down.shape
    nf = FF // tf
    dw_b3 = dw_b.reshape(L, 1, 2 * FF)
    vmem = 3 * tm * D * 4 + tm * D * 2 + 2 * (tm + 8) * tf * 4 \
        + 2 * 2 * D * tf * 2 + 2 * tf * D * 2 + 6 * tm * tf * 4 + (4 << 20)
    return pl.pallas_call(
        functools.partial(_ffn_kernel, tm=tm, tf=tf),
        grid=(B, S // tm, nf),
        in_specs=[pl.BlockSpec((1, tm, D), lambda b, i, j: (b, i, 0), pipeline_mode=pl.Buffered(1)),
                  pl.BlockSpec((1, 1, D), lambda b, i, j: (b, 0, 0)),
                  pl.BlockSpec((1, 1, D), lambda b, i, j: (b, 0, 0)),
                  pl.BlockSpec((1, 1, D), lambda b, i, j: (b, 0, 0)),
                  pl.BlockSpec((1, 1, D), lambda b, i, j: (l, 0, 0)),
                  pl.BlockSpec((1, D, tf), lambda b, i, j: (l, 0, j)),
                  pl.BlockSpec((1, D, tf), lambda b, i, j: (l, 0, nf + j)),
                  pl.BlockSpec((1, FFN_WIDTH, tf), lambda b, i, j: (l, 0, j)),
                  pl.BlockSpec((1, FFN_WIDTH, tf), lambda b, i, j: (l, 0, nf + j)),
                  pl.BlockSpec((1, 1, tf), lambda b, i, j: (l, 0, j)),
                  pl.BlockSpec((1, 1, tf), lambda b, i, j: (l, 0, nf + j)),
                  pl.BlockSpec((1, tf, D), lambda b, i, j: (l, j, 0))],
        out_specs=pl.BlockSpec((1, tm, D), lambda b, i, j: (b, i, 0)),
        out_shape=jax.ShapeDtypeStruct((B, S, D), F32),
        scratch_shapes=[pltpu.VMEM((tm, D), BF16),
                        pltpu.VMEM((2, tm + V7X_SUBLANES, tf), F32),
                        pltpu.VMEM((nf, 2, V7X_SUBLANES, tf), F32)],
        compiler_params=_params(("parallel", "arbitrary", "arbitrary"), vmem),
        name="conv_ffn",
    )(x, sh, sc, gate, g, w_up, w_up, dw_w, dw_w, dw_b3, dw_b3, w_down)


def _final_norm_kernel(x_ref, g_ref, o_ref):
    x = x_ref[0]
    ms = jnp.mean(x * x, axis=-1, keepdims=True)
    o_ref[0] = x * lax.rsqrt(ms + RMS_EPS) * g_ref[...]


def _final_norm_call(x, g, *, tm=256):
    B, S, D = x.shape
    return pl.pallas_call(
        _final_norm_kernel,
        grid=(B, S // tm),
        in_specs=[pl.BlockSpec((1, tm, D), lambda b, i: (b, i, 0)),
                  pl.BlockSpec((1, D), lambda b, i: (0, 0))],
        out_specs=pl.BlockSpec((1, tm, D), lambda b, i: (b, i, 0)),
        out_shape=jax.ShapeDtypeStruct((B, S, D), F32),
        compiler_params=_params(("parallel", "parallel"), 32 << 20),
        name="final_norm",
    )(x, g.reshape(1, D))


def kernel(x, c, ada_w, ada_b, mix_norm_g, w_in, b_forget, conf_dw_w, conf_dw_b, conf_ln_g,
           conf_ln_b, sc_dw_w, w_out, ffn_norm_g, w_up, ffn_dw_w, ffn_dw_b, w_down, final_norm_g):
    B, S, D = x.shape
    L = ada_w.shape[0]

    f0 = 3 * D_ATTN
    w_main = jnp.concatenate([w_in[..., :f0], w_in[..., f0 + N_HEADS:]], axis=-1).astype(BF16)
    w_f = jnp.pad(w_in[..., f0:f0 + N_HEADS], ((0, 0), (0, 0), (0, V7X_LANES - N_HEADS))).astype(BF16)
    col_scale = jnp.where(jnp.arange(w_main.shape[-1]) < D_ATTN, HEAD_DIM ** -0.5 * LOG2E, 1.0)
    col_scale = col_scale.astype(F32).reshape(1, -1)
    b_f = jnp.pad(b_forget, ((0, 0), (0, V7X_LANES - N_HEADS))).reshape(L, 1, V7X_LANES)
    w_out_b = w_out.astype(BF16)
    w_up_b = w_up.astype(BF16)
    w_down_b = w_down.astype(BF16)

    mix_g = mix_norm_g.reshape(L, 1, D)
    ffn_g = ffn_norm_g.reshape(L, 1, D)
    c_pad = jnp.pad(c, ((0, V7X_SUBLANES - B), (0, 0)))
    ada = _ada_call(c_pad, ada_w, ada_b)[:, :B]
    ada = ada.reshape(L, B, N_ADA, 1, D)

    for l in range(L):
        sh_m, sc_m, g_m, sh_f, sc_f, g_f = (ada[l, :, k] for k in range(N_ADA))
        proj, logf = _mix_in_call(x, sh_m, sc_m, mix_g, w_main, col_scale, w_f, b_f, l)
        nf = _cumsum_call(logf)
        attn = _attn_call(proj, nf)
        conv = _conv_call(proj, conf_dw_w, conf_dw_b, conf_ln_g, conf_ln_b, sc_dw_w, l)
        x = _mix_out_call(attn, conv, w_out_b, x, g_m, l)
        x = _ffn_call(x, sh_f, sc_f, g_f, ffn_g, w_up_b, ffn_dw_w, ffn_dw_b, w_down_b, l)
    return _final_norm_call(x, final_norm_g)
```

```python
import functools

import jax
import jax.numpy as jnp
from jax import lax
from jax.experimental import pallas as pl
from jax.experimental.pallas import tpu as pltpu

F32 = jnp.float32
BF16 = jnp.bfloat16

V7X_VMEM_BYTES = 64 * 1024 * 1024
V7X_LANES = 128
V7X_SUBLANES = 8

HEAD_DIM = 128
N_HEADS = 8
D_ATTN = N_HEADS * HEAD_DIM
D_CONF = 512
D_SCONV = 512
CONF_WIDTH = 31
SCONV_WIDTH = 3
FFN_WIDTH = 3
N_ADA = 6
RMS_EPS = 1e-6
LN_EPS = 1e-5
CONF_HALO = 32
SCONV_HALO = 16
FFN_TILE = 512
LOG2E = 1.4426950408889634


def _params(semantics, vmem_bytes):
    return pltpu.CompilerParams(dimension_semantics=semantics,
                                vmem_limit_bytes=min(vmem_bytes, V7X_VMEM_BYTES - (6 << 20)))


def _ada_kernel(c_ref, w_ref, b_ref, o_ref):
    c = c_ref[...]
    ca = (c * jax.nn.sigmoid(c)).astype(BF16)
    w = w_ref[0].astype(BF16)
    o_ref[0] = jnp.dot(ca, w, preferred_element_type=F32) + b_ref[0]


def _ada_call(c_pad, ada_w, ada_b):
    L, D, N = ada_w.shape
    tn = 1024
    return pl.pallas_call(
        _ada_kernel,
        grid=(L, N // tn),
        in_specs=[pl.BlockSpec((V7X_SUBLANES, D), lambda l, j: (0, 0)),
                  pl.BlockSpec((1, D, tn), lambda l, j: (l, 0, j)),
                  pl.BlockSpec((1, 1, tn), lambda l, j: (l, 0, j))],
        out_specs=pl.BlockSpec((1, V7X_SUBLANES, tn), lambda l, j: (l, 0, j)),
        out_shape=jax.ShapeDtypeStruct((L, V7X_SUBLANES, N), F32),
        compiler_params=_params(("parallel", "arbitrary"), 32 << 20),
        name="ada",
    )(c_pad, ada_w, ada_b.reshape(L, 1, N))


def _norm_mod_rows(x_ref, g_ref, sh_ref, sc_ref, h_scr, tm, rows=256):
    gs = g_ref[0] * (1.0 + sc_ref[0])
    sh = sh_ref[0]

    def body(r, carry):
        sl = pl.ds(pl.multiple_of(r * rows, rows), rows)
        x = x_ref[0, sl, :]
        ms = jnp.mean(x * x, axis=-1, keepdims=True)
        y = x * lax.rsqrt(ms + RMS_EPS)
        h_scr[sl, :] = (y * gs + sh).astype(BF16)
        return carry

    lax.fori_loop(0, tm // rows, body, 0)


def _mix_in_kernel(x_ref, sh_ref, sc_ref, g_ref, w_ref, cs_ref, wf_ref, bf_ref,
                   proj_ref, vt_ref, logf_ref, h_scr, *, tm, tn):
    _norm_mod_rows(x_ref, g_ref, sh_ref, sc_ref, h_scr, tm)
    h = h_scr[...]
    z = jnp.dot(h, wf_ref[0], preferred_element_type=F32) + bf_ref[0]
    logf_ref[0] = jnp.minimum(z, 0.0) - jnp.log1p(jnp.exp(-jnp.abs(z)))
    per = tn // V7X_LANES
    nchunk = proj_ref.shape[1]
    for c in range(w_ref.shape[-1] // tn):
        cols = slice(c * tn, (c + 1) * tn)
        r = jnp.dot(h, w_ref[0, :, cols], preferred_element_type=F32) * cs_ref[:, cols]
        for k in range(per):
            chunk = r[:, k * V7X_LANES:(k + 1) * V7X_LANES]
            idx = c * per + k
            if idx < nchunk:
                proj_ref[0, idx] = chunk.astype(BF16)
            else:
                vt_ref[0, idx - nchunk] = chunk.T.astype(BF16)


def _mix_in_call(x, sh, sc, g, w_main, col_scale, w_f, b_f, l, *, tm=512, tn=512):
    B, S, D = x.shape
    N = w_main.shape[-1]
    nchunk = N // V7X_LANES - N_HEADS
    vmem = 2 * tm * D * 4 + tm * D * 2 + D * N * 2 + 2 * D * V7X_LANES * 2 \
        + 2 * tm * N * 2 + 2 * tm * V7X_LANES * 4 + 4 * tm * tn * 4 + (4 << 20)
    return pl.pallas_call(
        functools.partial(_mix_in_kernel, tm=tm, tn=tn),
        grid=(B, S // tm),
        in_specs=[pl.BlockSpec((1, tm, D), lambda b, i: (b, i, 0)),
                  pl.BlockSpec((1, 1, D), lambda b, i: (b, 0, 0)),
                  pl.BlockSpec((1, 1, D), lambda b, i: (b, 0, 0)),
                  pl.BlockSpec((1, 1, D), lambda b, i: (l, 0, 0)),
                  pl.BlockSpec((1, D, N), lambda b, i: (l, 0, 0), pipeline_mode=pl.Buffered(1)),
                  pl.BlockSpec((1, N), lambda b, i: (0, 0)),
                  pl.BlockSpec((1, D, V7X_LANES), lambda b, i: (l, 0, 0)),
                  pl.BlockSpec((1, 1, V7X_LANES), lambda b, i: (l, 0, 0))],
        out_specs=[pl.BlockSpec((1, nchunk, tm, V7X_LANES), lambda b, i: (b, 0, i, 0)),
                   pl.BlockSpec((1, N_HEADS, HEAD_DIM, tm), lambda b, i: (b, 0, 0, i)),
                   pl.BlockSpec((1, tm, V7X_LANES), lambda b, i: (b, i, 0))],
        out_shape=[jax.ShapeDtypeStruct((B, nchunk, S, V7X_LANES), BF16),
                   jax.ShapeDtypeStruct((B, N_HEADS, HEAD_DIM, S), BF16),
                   jax.ShapeDtypeStruct((B, S, V7X_LANES), F32)],
        scratch_shapes=[pltpu.VMEM((tm, D), BF16)],
        compiler_params=_params(("parallel", "parallel"), vmem),
        name="mix_in",
    )(x, sh, sc, g, w_main, col_scale, w_f, b_f)


def _cumsum_kernel(l_ref, o_ref, carry_ref, *, ts):
    i = pl.program_id(1)

    @pl.when(i == 0)
    def _():
        carry_ref[...] = jnp.zeros_like(carry_ref)

    lf = l_ref[0]
    row = lax.broadcasted_iota(jnp.int32, (ts, ts), 0)
    col = lax.broadcasted_iota(jnp.int32, (ts, ts), 1)
    tri = jnp.where(col <= row, 1.0, 0.0).astype(BF16)
    hi = lf.astype(BF16)
    r1 = lf - hi.astype(F32)
    mid = r1.astype(BF16)
    lo = (r1 - mid.astype(F32)).astype(BF16)
    f = (jnp.dot(tri, hi, preferred_element_type=F32)
         + jnp.dot(tri, mid, preferred_element_type=F32)
         + jnp.dot(tri, lo, preferred_element_type=F32)
         + carry_ref[0:1, :])
    carry_ref[...] = jnp.broadcast_to(f[ts - 1:ts, :], carry_ref.shape)
    nf = f * (-LOG2E)
    lane = lax.broadcasted_iota(jnp.int32, (ts, V7X_LANES), 1)
    for hd in range(N_HEADS):
        v = jnp.broadcast_to(nf[:, hd:hd + 1], (ts, V7X_LANES))
        p0 = v.astype(BF16).astype(F32)
        p1 = (v - p0).astype(BF16).astype(F32)
        p2 = (v - p0 - p1).astype(BF16).astype(F32)
        pieces = jnp.where(lane == 0, p0, jnp.where(lane == 1, p1, jnp.where(lane == 2, p2, 0.0)))
        o_ref[0, hd] = pieces.astype(BF16)


def _cumsum_call(logf, *, ts=512):
    B, S, _ = logf.shape
    return pl.pallas_call(
        functools.partial(_cumsum_kernel, ts=ts),
        grid=(B, S // ts),
        in_specs=[pl.BlockSpec((1, ts, V7X_LANES), lambda b, i: (b, i, 0))],
        out_specs=pl.BlockSpec((1, N_HEADS, ts, V7X_LANES), lambda b, i: (b, 0, i, 0)),
        out_shape=jax.ShapeDtypeStruct((B, N_HEADS, S, V7X_LANES), BF16),
        scratch_shapes=[pltpu.VMEM((V7X_SUBLANES, V7X_LANES), F32)],
        compiler_params=_params(("parallel", "arbitrary"), 32 << 20),
        name="forget_cumsum",
    )(logf)


N_BIAS_LANES = 3


def _attn_kernel(q_ref, k_ref, kb_ref, vt_ref, o_ref, s0, s1, m_scr, l_scr, acc_scr, *, tk, tg, nsub):
    i = pl.program_id(2)
    slots = (s0, s1)
    lane = lax.broadcasted_iota(jnp.int32, (tg, V7X_LANES), 1)
    ones = jnp.where(lane < N_BIAS_LANES, 1.0, 0.0).astype(BF16)
    qa = [jnp.concatenate([q_ref[0, 0, a * tg:(a + 1) * tg, :], ones], axis=1) for a in range(nsub)]

    def keys_of(a, diag):
        return (a + 1) * tg if diag else tk

    def scores(tile, slot, diag=False):
        start = pl.multiple_of(tile * tk, tk)
        ka = jnp.concatenate([k_ref[0, 0, pl.ds(start, tk), :], kb_ref[0, 0, pl.ds(start, tk), :]], axis=1)
        for a in range(nsub):
            n = keys_of(a, diag)
            slots[slot][a, 0:n, :] = lax.dot_general(ka[0:n], qa[a], (((1,), (1,)), ((), ())),
                                                     preferred_element_type=F32)

    def softmax_pv(tile, slot, diag=False):
        start = pl.multiple_of(tile * tk, tk)
        ps, alphas = [], []
        for a in range(nsub):
            n = keys_of(a, diag)
            st = slots[slot][a, 0:n, :]
            if diag:
                key = lax.broadcasted_iota(jnp.int32, (n, tg), 0)
                qry = lax.broadcasted_iota(jnp.int32, (n, tg), 1)
                st = jnp.where(key - a * tg <= qry, st, -jnp.inf)
            m = m_scr[a, 0:1, :]
            m_new = jnp.maximum(m, jnp.max(st, axis=0, keepdims=True))
            p = jnp.exp2(st - m_new)
            alpha = jnp.exp2(m - m_new)
            m_scr[a, 0:1, :] = m_new
            l_scr[a, 0:1, :] = alpha * l_scr[a, 0:1, :] + jnp.sum(p, axis=0, keepdims=True)
            ps.append(p.astype(BF16))
            alphas.append(alpha)
        for a in range(nsub):
            n = keys_of(a, diag)
            vt = vt_ref[0, 0, :, pl.ds(start, n)]
            acc_scr[a] = alphas[a] * acc_scr[a] + jnp.dot(vt, ps[a], preferred_element_type=F32)

    m_scr[...] = jnp.full(m_scr.shape, -jnp.inf, F32)
    l_scr[...] = jnp.zeros_like(l_scr)
    acc_scr[...] = jnp.zeros_like(acc_scr)

    scores(i, 0, diag=True)

    @pl.when(i >= 1)
    def _():
        scores(i - 1, 1)

    softmax_pv(i, 0, diag=True)

    def pair(jj, carry):
        n = 2 * jj + 1
        scores(i - n - 1, 0)
        softmax_pv(i - n, 1)
        scores(i - n - 2, 1)
        softmax_pv(i - n - 1, 0)
        return carry

    lax.fori_loop(0, jnp.maximum(i - 1, 0) // 2, pair, 0)

    @pl.when(i % 2 == 1)
    def _():
        softmax_pv(0, 1)

    @pl.when(jnp.logical_and(i % 2 == 0, i >= 2))
    def _():
        scores(0, 0)
        softmax_pv(1, 1)
        softmax_pv(0, 0)

    for a in range(nsub):
        o_ref[0, a * tg:(a + 1) * tg, :] = (acc_scr[a] / l_scr[a, 0:1, :]).T.astype(BF16)


def _attn_call(proj, kbias, vt, *, tk=1024, tg=256):
    B, _, S, _ = proj.shape
    nsub = tk // tg
    return pl.pallas_call(
        functools.partial(_attn_kernel, tk=tk, tg=tg, nsub=nsub),
        grid=(B, N_HEADS, S // tk),
        in_specs=[pl.BlockSpec((1, 1, tk, HEAD_DIM), lambda b, h, i: (b, h, i, 0)),
                  pl.BlockSpec((1, 1, S, HEAD_DIM), lambda b, h, i: (b, N_HEADS + h, 0, 0)),
                  pl.BlockSpec((1, 1, S, V7X_LANES), lambda b, h, i: (b, h, 0, 0)),
                  pl.BlockSpec((1, 1, HEAD_DIM, S), lambda b, h, i: (b, h, 0, 0))],
        out_specs=pl.BlockSpec((1, tk, HEAD_DIM), lambda b, h, i: (b, i, h)),
        out_shape=jax.ShapeDtypeStruct((B, S, D_ATTN), BF16),
        scratch_shapes=[pltpu.VMEM((nsub, tk, tg), F32),
                        pltpu.VMEM((nsub, tk, tg), F32),
                        pltpu.VMEM((nsub, V7X_SUBLANES, tg), F32),
                        pltpu.VMEM((nsub, V7X_SUBLANES, tg), F32),
                        pltpu.VMEM((nsub, HEAD_DIM, tg), F32)],
        compiler_params=_params(("parallel", "parallel", "arbitrary"), 48 << 20),
        name="fox_attn",
    )(proj, proj, kbias, vt)


def _conv_kernel(cv_ref, cg_ref, cvh_ref, cgh_ref, sx_ref, sb_ref, sc_ref, sxh_ref, sch_ref,
                 cw_ref, cb_ref, lg_ref, lb_ref, sw_ref, o_ref, cbuf, sbuf, *, ts, rows):
    first = pl.program_id(1) == 0
    nl = D_CONF // V7X_LANES
    for c in range(nl):
        cols = slice(c * V7X_LANES, (c + 1) * V7X_LANES)
        gh = cvh_ref[0, c].astype(F32) * jax.nn.sigmoid(cgh_ref[0, c].astype(F32))
        cbuf[0:CONF_HALO, cols] = jnp.where(first, 0.0, gh)
        cbuf[CONF_HALO:, cols] = cv_ref[0, c].astype(F32) * jax.nn.sigmoid(cg_ref[0, c].astype(F32))
        zh = sch_ref[0, c].astype(F32) * sxh_ref[0, c].astype(F32)
        sbuf[0:SCONV_HALO, cols] = jnp.where(first, 0.0, zh)
        sbuf[SCONV_HALO:, cols] = sc_ref[0, c].astype(F32) * sx_ref[0, c].astype(F32)

    cb = cb_ref[0]
    lg = lg_ref[0]
    lb = lb_ref[0]
    for r in range(ts // rows):
        r0 = r * rows
        acc = jnp.zeros((rows, D_CONF), F32)
        for k in range(CONF_WIDTH):
            o = r0 + CONF_HALO - (CONF_WIDTH - 1) + k
            acc = acc + cw_ref[0, k:k + 1, :] * cbuf[o:o + rows, :]
        acc = acc + cb
        mu = jnp.mean(acc, axis=-1, keepdims=True)
        xc = acc - mu
        var = jnp.mean(xc * xc, axis=-1, keepdims=True)
        y = xc * lax.rsqrt(var + LN_EPS) * lg + lb
        o_ref[0, r0:r0 + rows, 0:D_CONF] = (y * jax.nn.sigmoid(y)).astype(BF16)

        sacc = jnp.zeros((rows, D_SCONV), F32)
        for k in range(SCONV_WIDTH):
            o = r0 + SCONV_HALO - (SCONV_WIDTH - 1) + k
            sacc = sacc + sw_ref[0, k:k + 1, :] * sbuf[o:o + rows, :]
        for c in range(nl):
            cols = slice(c * V7X_LANES, (c + 1) * V7X_LANES)
            gated = sb_ref[0, c, r0:r0 + rows, :].astype(F32) * sacc[:, cols]
            o_ref[0, r0:r0 + rows, D_CONF + c * V7X_LANES:D_CONF + (c + 1) * V7X_LANES] = gated.astype(BF16)


def _conv_call(proj, conf_dw_w, conf_dw_b, conf_ln_g, conf_ln_b, sc_dw_w, l, *, ts=256, rows=32):
    B, _, S, _ = proj.shape
    L = conf_dw_w.shape[0]
    nl = D_CONF // V7X_LANES
    base = 2 * D_ATTN // V7X_LANES // nl

    def tile(g):
        return pl.BlockSpec((1, nl, ts, V7X_LANES), lambda b, i: (b, base + g, i, 0))

    def halo(g, n):
        return pl.BlockSpec((1, nl, n, V7X_LANES),
                            lambda b, i: (b, base + g, jnp.maximum(i * (ts // n) - 1, 0), 0))

    def wspec(k):
        return pl.BlockSpec((1, k, D_CONF), lambda b, i: (l, 0, 0))

    return pl.pallas_call(
        functools.partial(_conv_kernel, ts=ts, rows=rows),
        grid=(B, S // ts),
        in_specs=[tile(0), tile(1), halo(0, CONF_HALO), halo(1, CONF_HALO),
                  tile(2), tile(3), tile(4), halo(2, SCONV_HALO), halo(4, SCONV_HALO),
                  wspec(CONF_WIDTH), wspec(1), wspec(1), wspec(1), wspec(SCONV_WIDTH)],
        out_specs=pl.BlockSpec((1, ts, D_CONF + D_SCONV), lambda b, i: (b, i, 0)),
        out_shape=jax.ShapeDtypeStruct((B, S, D_CONF + D_SCONV), BF16),
        scratch_shapes=[pltpu.VMEM((ts + CONF_HALO, D_CONF), F32),
                        pltpu.VMEM((ts + SCONV_HALO, D_SCONV), F32)],
        compiler_params=_params(("parallel", "arbitrary"), 32 << 20),
        name="conv_mixers",
    )(proj, proj, proj, proj, proj, proj, proj, proj, proj,
      conf_dw_w, conf_dw_b.reshape(L, 1, D_CONF), conf_ln_g.reshape(L, 1, D_CONF),
      conf_ln_b.reshape(L, 1, D_CONF), sc_dw_w)


def _mix_out_kernel(a_ref, c_ref, w_ref, x_ref, g_ref, o_ref):
    ka = a_ref.shape[-1]
    r = jnp.dot(a_ref[0], w_ref[0, 0:ka, :], preferred_element_type=F32)
    r = r + jnp.dot(c_ref[0], w_ref[0, ka:, :], preferred_element_type=F32)
    o_ref[0] = x_ref[0] + g_ref[0] * r


def _mix_out_call(attn, conv, w_out, x, gate, l, *, tm=512):
    B, S, D = x.shape
    ka, kc = attn.shape[-1], conv.shape[-1]
    vmem = 2 * (ka + kc) * D * 2 + 2 * tm * (ka + kc) * 2 + 4 * tm * D * 4 + 2 * tm * D * 4 + (8 << 20)
    return pl.pallas_call(
        _mix_out_kernel,
        grid=(B, S // tm),
        in_specs=[pl.BlockSpec((1, tm, ka), lambda b, i: (b, i, 0)),
                  pl.BlockSpec((1, tm, kc), lambda b, i: (b, i, 0)),
                  pl.BlockSpec((1, ka + kc, D), lambda b, i: (l, 0, 0)),
                  pl.BlockSpec((1, tm, D), lambda b, i: (b, i, 0)),
                  pl.BlockSpec((1, 1, D), lambda b, i: (b, 0, 0))],
        out_specs=pl.BlockSpec((1, tm, D), lambda b, i: (b, i, 0)),
        out_shape=jax.ShapeDtypeStruct((B, S, D), F32),
        compiler_params=_params(("parallel", "parallel"), vmem),
        name="mix_out",
    )(attn, conv, w_out, x, gate)


def _ffn_kernel(x_ref, sh_ref, sc_ref, gt_ref, g_ref, wg_ref, wv_ref, dwg_ref, dwv_ref,
                dbg_ref, dbv_ref, wd_ref, o_ref, h_scr, ubuf, carry_ref, *, tm, tf):
    i = pl.program_id(1)
    j = pl.program_id(2)
    pad = V7X_SUBLANES

    @pl.when(j == 0)
    def _():
        _norm_mod_rows(x_ref, g_ref, sh_ref, sc_ref, h_scr, tm)
        o_ref[...] = jnp.zeros_like(o_ref)

    h = h_scr[...]

    def conv_branch(idx, w_ref, dw_ref, db_ref):
        u = jnp.dot(h, w_ref[0, 0], preferred_element_type=F32)
        prev = carry_ref[j, idx]
        ubuf[idx, 0:pad, :] = jnp.where(i == 0, 0.0, prev)
        ubuf[idx, pad:, :] = u
        carry_ref[j, idx] = u[tm - pad:, :]
        return (dw_ref[0, 2:3, :] * u
                + dw_ref[0, 1:2, :] * ubuf[idx, pad - 1:pad - 1 + tm, :]
                + dw_ref[0, 0:1, :] * ubuf[idx, pad - 2:pad - 2 + tm, :]
                + db_ref[0])

    yg = conv_branch(0, wg_ref, dwg_ref, dbg_ref)
    yv = conv_branch(1, wv_ref, dwv_ref, dbv_ref)
    act = (yg * jax.nn.sigmoid(yg) * yv).astype(BF16)
    o_ref[0] += jnp.dot(act, wd_ref[0], preferred_element_type=F32)

    @pl.when(j == pl.num_programs(2) - 1)
    def _():
        o_ref[0] = x_ref[0] + gt_ref[0] * o_ref[0]


def _ffn_call(x, sh, sc, gate, g, w_up, dw_w, dw_b, w_down, l, *, tm=1024):
    B, S, D = x.shape
    L, FF, _ = w_down.shape
    tf = w_up.shape[-1]
    nf = FF // tf
    dw_b3 = dw_b.reshape(L, 1, 2 * FF)
    vmem = 3 * tm * D * 4 + tm * D * 2 + 2 * (tm + 8) * tf * 4 \
        + 2 * 2 * D * tf * 2 + 2 * tf * D * 2 + 6 * tm * tf * 4 + (4 << 20)
    return pl.pallas_call(
        functools.partial(_ffn_kernel, tm=tm, tf=tf),
        grid=(B, S // tm, nf),
        in_specs=[pl.BlockSpec((1, tm, D), lambda b, i, j: (b, i, 0), pipeline_mode=pl.Buffered(1)),
                  pl.BlockSpec((1, 1, D), lambda b, i, j: (b, 0, 0)),
                  pl.BlockSpec((1, 1, D), lambda b, i, j: (b, 0, 0)),
                  pl.BlockSpec((1, 1, D), lambda b, i, j: (b, 0, 0)),
                  pl.BlockSpec((1, 1, D), lambda b, i, j: (l, 0, 0)),
                  pl.BlockSpec((1, 1, D, tf), lambda b, i, j: (l, j, 0, 0)),
                  pl.BlockSpec((1, 1, D, tf), lambda b, i, j: (l, nf + j, 0, 0)),
                  pl.BlockSpec((1, FFN_WIDTH, tf), lambda b, i, j: (l, 0, j)),
                  pl.BlockSpec((1, FFN_WIDTH, tf), lambda b, i, j: (l, 0, nf + j)),
                  pl.BlockSpec((1, 1, tf), lambda b, i, j: (l, 0, j)),
                  pl.BlockSpec((1, 1, tf), lambda b, i, j: (l, 0, nf + j)),
                  pl.BlockSpec((1, tf, D), lambda b, i, j: (l, j, 0))],
        out_specs=pl.BlockSpec((1, tm, D), lambda b, i, j: (b, i, 0)),
        out_shape=jax.ShapeDtypeStruct((B, S, D), F32),
        scratch_shapes=[pltpu.VMEM((tm, D), BF16),
                        pltpu.VMEM((2, tm + V7X_SUBLANES, tf), F32),
                        pltpu.VMEM((nf, 2, V7X_SUBLANES, tf), F32)],
        compiler_params=_params(("parallel", "arbitrary", "arbitrary"), vmem),
        name="conv_ffn",
    )(x, sh, sc, gate, g, w_up, w_up, dw_w, dw_w, dw_b3, dw_b3, w_down)


def _final_norm_kernel(x_ref, g_ref, o_ref):
    x = x_ref[0]
    ms = jnp.mean(x * x, axis=-1, keepdims=True)
    o_ref[0] = x * lax.rsqrt(ms + RMS_EPS) * g_ref[...]


def _final_norm_call(x, g, *, tm=256):
    B, S, D = x.shape
    return pl.pallas_call(
        _final_norm_kernel,
        grid=(B, S // tm),
        in_specs=[pl.BlockSpec((1, tm, D), lambda b, i: (b, i, 0)),
                  pl.BlockSpec((1, D), lambda b, i: (0, 0))],
        out_specs=pl.BlockSpec((1, tm, D), lambda b, i: (b, i, 0)),
        out_shape=jax.ShapeDtypeStruct((B, S, D), F32),
        compiler_params=_params(("parallel", "parallel"), 32 << 20),
        name="final_norm",
    )(x, g.reshape(1, D))


def kernel(x, c, ada_w, ada_b, mix_norm_g, w_in, b_forget, conf_dw_w, conf_dw_b, conf_ln_g,
           conf_ln_b, sc_dw_w, w_out, ffn_norm_g, w_up, ffn_dw_w, ffn_dw_b, w_down, final_norm_g):
    B, S, D = x.shape
    L = ada_w.shape[0]

    f0 = 3 * D_ATTN
    v0 = 2 * D_ATTN
    w_main = jnp.concatenate([w_in[..., :v0], w_in[..., f0 + N_HEADS:], w_in[..., v0:f0]], axis=-1).astype(BF16)
    w_f = jnp.pad(w_in[..., f0:f0 + N_HEADS], ((0, 0), (0, 0), (0, V7X_LANES - N_HEADS))).astype(BF16)
    col_scale = jnp.where(jnp.arange(w_main.shape[-1]) < D_ATTN, HEAD_DIM ** -0.5 * LOG2E, 1.0)
    col_scale = col_scale.astype(F32).reshape(1, -1)
    b_f = jnp.pad(b_forget, ((0, 0), (0, V7X_LANES - N_HEADS))).reshape(L, 1, V7X_LANES)
    w_out_b = w_out.astype(BF16)
    w_up_b = w_up.astype(BF16).reshape(L, D, -1, FFN_TILE).transpose(0, 2, 1, 3)
    w_down_b = w_down.astype(BF16)

    mix_g = mix_norm_g.reshape(L, 1, D)
    ffn_g = ffn_norm_g.reshape(L, 1, D)
    c_pad = jnp.pad(c, ((0, V7X_SUBLANES - B), (0, 0)))
    ada = _ada_call(c_pad, ada_w, ada_b)[:, :B]
    ada = ada.reshape(L, B, N_ADA, 1, D)

    for l in range(L):
        sh_m, sc_m, g_m, sh_f, sc_f, g_f = (ada[l, :, k] for k in range(N_ADA))
        proj, vt, logf = _mix_in_call(x, sh_m, sc_m, mix_g, w_main, col_scale, w_f, b_f, l)
        kbias = _cumsum_call(logf)
        attn = _attn_call(proj, kbias, vt)
        conv = _conv_call(proj, conf_dw_w, conf_dw_b, conf_ln_g, conf_ln_b, sc_dw_w, l)
        x = _mix_out_call(attn, conv, w_out_b, x, g_m, l)
        x = _ffn_call(x, sh_f, sc_f, g_f, ffn_g, w_up_b, ffn_dw_w, ffn_dw_b, w_down_b, l)
    return _final_norm_call(x, final_norm_g)
```

```python
import functools

import jax
import jax.numpy as jnp
from jax import lax
from jax.experimental import pallas as pl
from jax.experimental.pallas import tpu as pltpu

F32 = jnp.float32
BF16 = jnp.bfloat16

V7X_VMEM_BYTES = 64 * 1024 * 1024
V7X_LANES = 128
V7X_SUBLANES = 8

HEAD_DIM = 128
N_HEADS = 8
D_ATTN = N_HEADS * HEAD_DIM
D_CONF = 512
D_SCONV = 512
CONF_WIDTH = 31
SCONV_WIDTH = 3
FFN_WIDTH = 3
N_ADA = 6
RMS_EPS = 1e-6
LN_EPS = 1e-5
CONF_HALO = 32
SCONV_HALO = 16
FFN_TILE = 512
FFN_SPLIT = 2
LOG2E = 1.4426950408889634


def _params(semantics, vmem_bytes):
    return pltpu.CompilerParams(dimension_semantics=semantics,
                                vmem_limit_bytes=min(vmem_bytes, V7X_VMEM_BYTES - (6 << 20)))


def _ada_kernel(c_ref, w_ref, b_ref, o_ref):
    c = c_ref[...]
    ca = (c * jax.nn.sigmoid(c)).astype(BF16)
    w = w_ref[0].astype(BF16)
    o_ref[0] = jnp.dot(ca, w, preferred_element_type=F32) + b_ref[0]


def _ada_call(c_pad, ada_w, ada_b):
    L, D, N = ada_w.shape
    tn = 1024
    return pl.pallas_call(
        _ada_kernel,
        grid=(L, N // tn),
        in_specs=[pl.BlockSpec((V7X_SUBLANES, D), lambda l, j: (0, 0)),
                  pl.BlockSpec((1, D, tn), lambda l, j: (l, 0, j)),
                  pl.BlockSpec((1, 1, tn), lambda l, j: (l, 0, j))],
        out_specs=pl.BlockSpec((1, V7X_SUBLANES, tn), lambda l, j: (l, 0, j)),
        out_shape=jax.ShapeDtypeStruct((L, V7X_SUBLANES, N), F32),
        compiler_params=_params(("parallel", "arbitrary"), 32 << 20),
        name="ada",
    )(c_pad, ada_w, ada_b.reshape(L, 1, N))


def _norm_mod_rows(x_ref, g_ref, sh_ref, sc_ref, h_scr, tm, rows=256):
    gs = g_ref[0] * (1.0 + sc_ref[0])
    sh = sh_ref[0]

    def body(r, carry):
        sl = pl.ds(pl.multiple_of(r * rows, rows), rows)
        x = x_ref[0, sl, :]
        ms = jnp.mean(x * x, axis=-1, keepdims=True)
        y = x * lax.rsqrt(ms + RMS_EPS)
        h_scr[sl, :] = (y * gs + sh).astype(BF16)
        return carry

    lax.fori_loop(0, tm // rows, body, 0)


def _mix_in_kernel(x_ref, sh_ref, sc_ref, g_ref, w_ref, cs_ref, wf_ref, bf_ref,
                   proj_ref, vt_ref, logf_ref, h_scr, *, tm, tn):
    _norm_mod_rows(x_ref, g_ref, sh_ref, sc_ref, h_scr, tm)
    h = h_scr[...]
    z = jnp.dot(h, wf_ref[0], preferred_element_type=F32) + bf_ref[0]
    logf_ref[0] = jnp.minimum(z, 0.0) - jnp.log1p(jnp.exp(-jnp.abs(z)))
    per = tn // V7X_LANES
    nchunk = proj_ref.shape[1]
    for c in range(w_ref.shape[-1] // tn):
        cols = slice(c * tn, (c + 1) * tn)
        r = jnp.dot(h, w_ref[0, :, cols], preferred_element_type=F32) * cs_ref[:, cols]
        for k in range(per):
            chunk = r[:, k * V7X_LANES:(k + 1) * V7X_LANES]
            idx = c * per + k
            if idx < nchunk:
                proj_ref[0, idx] = chunk.astype(BF16)
            else:
                vt_ref[0, idx - nchunk] = chunk.T.astype(BF16)


def _mix_in_call(x, sh, sc, g, w_main, col_scale, w_f, b_f, l, *, tm=512, tn=512):
    B, S, D = x.shape
    N = w_main.shape[-1]
    nchunk = N // V7X_LANES - N_HEADS
    vmem = 2 * tm * D * 4 + tm * D * 2 + D * N * 2 + 2 * D * V7X_LANES * 2 \
        + 2 * tm * N * 2 + 2 * tm * V7X_LANES * 4 + 4 * tm * tn * 4 + (4 << 20)
    return pl.pallas_call(
        functools.partial(_mix_in_kernel, tm=tm, tn=tn),
        grid=(B, S // tm),
        in_specs=[pl.BlockSpec((1, tm, D), lambda b, i: (b, i, 0)),
                  pl.BlockSpec((1, 1, D), lambda b, i: (b, 0, 0)),
                  pl.BlockSpec((1, 1, D), lambda b, i: (b, 0, 0)),
                  pl.BlockSpec((1, 1, D), lambda b, i: (l, 0, 0)),
                  pl.BlockSpec((1, D, N), lambda b, i: (l, 0, 0), pipeline_mode=pl.Buffered(1)),
                  pl.BlockSpec((1, N), lambda b, i: (0, 0)),
                  pl.BlockSpec((1, D, V7X_LANES), lambda b, i: (l, 0, 0)),
                  pl.BlockSpec((1, 1, V7X_LANES), lambda b, i: (l, 0, 0))],
        out_specs=[pl.BlockSpec((1, nchunk, tm, V7X_LANES), lambda b, i: (b, 0, i, 0)),
                   pl.BlockSpec((1, N_HEADS, HEAD_DIM, tm), lambda b, i: (b, 0, 0, i)),
                   pl.BlockSpec((1, tm, V7X_LANES), lambda b, i: (b, i, 0))],
        out_shape=[jax.ShapeDtypeStruct((B, nchunk, S, V7X_LANES), BF16),
                   jax.ShapeDtypeStruct((B, N_HEADS, HEAD_DIM, S), BF16),
                   jax.ShapeDtypeStruct((B, S, V7X_LANES), F32)],
        scratch_shapes=[pltpu.VMEM((tm, D), BF16)],
        compiler_params=_params(("parallel", "parallel"), vmem),
        name="mix_in",
    )(x, sh, sc, g, w_main, col_scale, w_f, b_f)


def _cumsum_kernel(l_ref, o_ref, carry_ref, *, ts):
    i = pl.program_id(1)

    @pl.when(i == 0)
    def _():
        carry_ref[...] = jnp.zeros_like(carry_ref)

    lf = l_ref[0]
    row = lax.broadcasted_iota(jnp.int32, (ts, ts), 0)
    col = lax.broadcasted_iota(jnp.int32, (ts, ts), 1)
    tri = jnp.where(col <= row, 1.0, 0.0).astype(BF16)
    hi = lf.astype(BF16)
    r1 = lf - hi.astype(F32)
    mid = r1.astype(BF16)
    lo = (r1 - mid.astype(F32)).astype(BF16)
    f = (jnp.dot(tri, hi, preferred_element_type=F32)
         + jnp.dot(tri, mid, preferred_element_type=F32)
         + jnp.dot(tri, lo, preferred_element_type=F32)
         + carry_ref[0:1, :])
    carry_ref[...] = jnp.broadcast_to(f[ts - 1:ts, :], carry_ref.shape)
    nf = f * (-LOG2E)
    lane = lax.broadcasted_iota(jnp.int32, (ts, V7X_LANES), 1)
    for hd in range(N_HEADS):
        v = jnp.broadcast_to(nf[:, hd:hd + 1], (ts, V7X_LANES))
        p0 = v.astype(BF16).astype(F32)
        p1 = (v - p0).astype(BF16).astype(F32)
        p2 = (v - p0 - p1).astype(BF16).astype(F32)
        pieces = jnp.where(lane == 0, p0, jnp.where(lane == 1, p1, jnp.where(lane == 2, p2, 0.0)))
        o_ref[0, hd] = pieces.astype(BF16)


def _cumsum_call(logf, *, ts=512):
    B, S, _ = logf.shape
    return pl.pallas_call(
        functools.partial(_cumsum_kernel, ts=ts),
        grid=(B, S // ts),
        in_specs=[pl.BlockSpec((1, ts, V7X_LANES), lambda b, i: (b, i, 0))],
        out_specs=pl.BlockSpec((1, N_HEADS, ts, V7X_LANES), lambda b, i: (b, 0, i, 0)),
        out_shape=jax.ShapeDtypeStruct((B, N_HEADS, S, V7X_LANES), BF16),
        scratch_shapes=[pltpu.VMEM((V7X_SUBLANES, V7X_LANES), F32)],
        compiler_params=_params(("parallel", "arbitrary"), 32 << 20),
        name="forget_cumsum",
    )(logf)


N_BIAS_LANES = 3


def _attn_kernel(q_ref, k_ref, kb_ref, vt_ref, o_ref, s0, s1, m_scr, l_scr, acc_scr, *, tk, tg, nsub):
    i = pl.program_id(2)
    slots = (s0, s1)
    lane = lax.broadcasted_iota(jnp.int32, (tg, V7X_LANES), 1)
    ones = jnp.where(lane < N_BIAS_LANES, 1.0, 0.0).astype(BF16)
    qa = [jnp.concatenate([q_ref[0, 0, a * tg:(a + 1) * tg, :], ones], axis=1) for a in range(nsub)]

    def keys_of(a, diag):
        return (a + 1) * tg if diag else tk

    def scores(tile, slot, diag=False):
        start = pl.multiple_of(tile * tk, tk)
        ka = jnp.concatenate([k_ref[0, 0, pl.ds(start, tk), :], kb_ref[0, 0, pl.ds(start, tk), :]], axis=1)
        for a in range(nsub):
            n = keys_of(a, diag)
            slots[slot][a, 0:n, :] = lax.dot_general(ka[0:n], qa[a], (((1,), (1,)), ((), ())),
                                                     preferred_element_type=F32)

    def softmax_pv(tile, slot, diag=False):
        start = pl.multiple_of(tile * tk, tk)
        ps, alphas = [], []
        for a in range(nsub):
            n = keys_of(a, diag)
            st = slots[slot][a, 0:n, :]
            if diag:
                key = lax.broadcasted_iota(jnp.int32, (n, tg), 0)
                qry = lax.broadcasted_iota(jnp.int32, (n, tg), 1)
                st = jnp.where(key - a * tg <= qry, st, -jnp.inf)
            m = m_scr[a, 0:1, :]
            m_new = jnp.maximum(m, jnp.max(st, axis=0, keepdims=True))
            p = jnp.exp2(st - m_new)
            alpha = jnp.exp2(m - m_new)
            m_scr[a, 0:1, :] = m_new
            l_scr[a, 0:1, :] = alpha * l_scr[a, 0:1, :] + jnp.sum(p, axis=0, keepdims=True)
            ps.append(p.astype(BF16))
            alphas.append(alpha)
        for a in range(nsub):
            n = keys_of(a, diag)
            vt = vt_ref[0, 0, :, pl.ds(start, n)]
            acc_scr[a] = alphas[a] * acc_scr[a] + jnp.dot(vt, ps[a], preferred_element_type=F32)

    m_scr[...] = jnp.full(m_scr.shape, -jnp.inf, F32)
    l_scr[...] = jnp.zeros_like(l_scr)
    acc_scr[...] = jnp.zeros_like(acc_scr)

    scores(i, 0, diag=True)

    @pl.when(i >= 1)
    def _():
        scores(i - 1, 1)

    softmax_pv(i, 0, diag=True)

    def pair(jj, carry):
        n = 2 * jj + 1
        scores(i - n - 1, 0)
        softmax_pv(i - n, 1)
        scores(i - n - 2, 1)
        softmax_pv(i - n - 1, 0)
        return carry

    lax.fori_loop(0, jnp.maximum(i - 1, 0) // 2, pair, 0)

    @pl.when(i % 2 == 1)
    def _():
        softmax_pv(0, 1)

    @pl.when(jnp.logical_and(i % 2 == 0, i >= 2))
    def _():
        scores(0, 0)
        softmax_pv(1, 1)
        softmax_pv(0, 0)

    for a in range(nsub):
        o_ref[0, a * tg:(a + 1) * tg, :] = (acc_scr[a] / l_scr[a, 0:1, :]).T.astype(BF16)


def _attn_call(proj, kbias, vt, *, tk=1024, tg=256):
    B, _, S, _ = proj.shape
    nsub = tk // tg
    return pl.pallas_call(
        functools.partial(_attn_kernel, tk=tk, tg=tg, nsub=nsub),
        grid=(B, N_HEADS, S // tk),
        in_specs=[pl.BlockSpec((1, 1, tk, HEAD_DIM), lambda b, h, i: (b, h, i, 0)),
                  pl.BlockSpec((1, 1, S, HEAD_DIM), lambda b, h, i: (b, N_HEADS + h, 0, 0)),
                  pl.BlockSpec((1, 1, S, V7X_LANES), lambda b, h, i: (b, h, 0, 0)),
                  pl.BlockSpec((1, 1, HEAD_DIM, S), lambda b, h, i: (b, h, 0, 0))],
        out_specs=pl.BlockSpec((1, tk, HEAD_DIM), lambda b, h, i: (b, i, h)),
        out_shape=jax.ShapeDtypeStruct((B, S, D_ATTN), BF16),
        scratch_shapes=[pltpu.VMEM((nsub, tk, tg), F32),
                        pltpu.VMEM((nsub, tk, tg), F32),
                        pltpu.VMEM((nsub, V7X_SUBLANES, tg), F32),
                        pltpu.VMEM((nsub, V7X_SUBLANES, tg), F32),
                        pltpu.VMEM((nsub, HEAD_DIM, tg), F32)],
        compiler_params=_params(("parallel", "parallel", "arbitrary"), 48 << 20),
        name="fox_attn",
    )(proj, proj, kbias, vt)


def _conv_kernel(cv_ref, cg_ref, cvh_ref, cgh_ref, sx_ref, sb_ref, sc_ref, sxh_ref, sch_ref,
                 cw_ref, cb_ref, lg_ref, lb_ref, sw_ref, o_ref, pbuf, sbuf, *, ts, rows):
    first = pl.program_id(1) == 0
    nl = D_CONF // V7X_LANES
    cbuf = pbuf.at[0]
    for c in range(nl):
        cols = slice(c * V7X_LANES, (c + 1) * V7X_LANES)
        gh = cvh_ref[0, c].astype(F32) * jax.nn.sigmoid(cgh_ref[0, c].astype(F32))
        cbuf[0:CONF_HALO, cols] = jnp.where(first, 0.0, gh)
        cbuf[CONF_HALO:, cols] = cv_ref[0, c].astype(F32) * jax.nn.sigmoid(cg_ref[0, c].astype(F32))
        zh = sch_ref[0, c].astype(F32) * sxh_ref[0, c].astype(F32)
        sbuf[0:SCONV_HALO, cols] = jnp.where(first, 0.0, zh)
        sbuf[SCONV_HALO:, cols] = sc_ref[0, c].astype(F32) * sx_ref[0, c].astype(F32)

    span = ts + CONF_HALO - V7X_SUBLANES
    for s in range(1, V7X_SUBLANES):
        pbuf[s, 0:span, :] = pbuf[0, s:s + span, :]

    cb = cb_ref[0]
    lg = lg_ref[0]
    lb = lb_ref[0]
    for r in range(ts // rows):
        r0 = r * rows
        acc = jnp.zeros((rows, D_CONF), F32)
        for k in range(CONF_WIDTH):
            o = CONF_HALO - (CONF_WIDTH - 1) + k
            ph, al = o % V7X_SUBLANES, r0 + o - o % V7X_SUBLANES
            acc = acc + cw_ref[0, k:k + 1, :] * pbuf[ph, al:al + rows, :]
        acc = acc + cb
        mu = jnp.mean(acc, axis=-1, keepdims=True)
        xc = acc - mu
        var = jnp.mean(xc * xc, axis=-1, keepdims=True)
        y = xc * lax.rsqrt(var + LN_EPS) * lg + lb
        o_ref[0, r0:r0 + rows, 0:D_CONF] = (y * jax.nn.sigmoid(y)).astype(BF16)

        sacc = jnp.zeros((rows, D_SCONV), F32)
        for k in range(SCONV_WIDTH):
            o = r0 + SCONV_HALO - (SCONV_WIDTH - 1) + k
            sacc = sacc + sw_ref[0, k:k + 1, :] * sbuf[o:o + rows, :]
        for c in range(nl):
            cols = slice(c * V7X_LANES, (c + 1) * V7X_LANES)
            gated = sb_ref[0, c, r0:r0 + rows, :].astype(F32) * sacc[:, cols]
            o_ref[0, r0:r0 + rows, D_CONF + c * V7X_LANES:D_CONF + (c + 1) * V7X_LANES] = gated.astype(BF16)


def _conv_call(proj, conf_dw_w, conf_dw_b, conf_ln_g, conf_ln_b, sc_dw_w, l, *, ts=256, rows=32):
    B, _, S, _ = proj.shape
    L = conf_dw_w.shape[0]
    nl = D_CONF // V7X_LANES
    base = 2 * D_ATTN // V7X_LANES // nl

    def tile(g):
        return pl.BlockSpec((1, nl, ts, V7X_LANES), lambda b, i: (b, base + g, i, 0))

    def halo(g, n):
        return pl.BlockSpec((1, nl, n, V7X_LANES),
                            lambda b, i: (b, base + g, jnp.maximum(i * (ts // n) - 1, 0), 0))

    def wspec(k):
        return pl.BlockSpec((1, k, D_CONF), lambda b, i: (l, 0, 0))

    return pl.pallas_call(
        functools.partial(_conv_kernel, ts=ts, rows=rows),
        grid=(B, S // ts),
        in_specs=[tile(0), tile(1), halo(0, CONF_HALO), halo(1, CONF_HALO),
                  tile(2), tile(3), tile(4), halo(2, SCONV_HALO), halo(4, SCONV_HALO),
                  wspec(CONF_WIDTH), wspec(1), wspec(1), wspec(1), wspec(SCONV_WIDTH)],
        out_specs=pl.BlockSpec((1, ts, D_CONF + D_SCONV), lambda b, i: (b, i, 0)),
        out_shape=jax.ShapeDtypeStruct((B, S, D_CONF + D_SCONV), BF16),
        scratch_shapes=[pltpu.VMEM((V7X_SUBLANES, ts + CONF_HALO, D_CONF), F32),
                        pltpu.VMEM((ts + SCONV_HALO, D_SCONV), F32)],
        compiler_params=_params(("parallel", "arbitrary"), 32 << 20),
        name="conv_mixers",
    )(proj, proj, proj, proj, proj, proj, proj, proj, proj,
      conf_dw_w, conf_dw_b.reshape(L, 1, D_CONF), conf_ln_g.reshape(L, 1, D_CONF),
      conf_ln_b.reshape(L, 1, D_CONF), sc_dw_w)


def _mix_out_kernel(a_ref, c_ref, w_ref, x_ref, g_ref, o_ref):
    ka = a_ref.shape[-1]
    r = jnp.dot(a_ref[0], w_ref[0, 0:ka, :], preferred_element_type=F32)
    r = r + jnp.dot(c_ref[0], w_ref[0, ka:, :], preferred_element_type=F32)
    o_ref[0] = x_ref[0] + g_ref[0] * r


def _mix_out_call(attn, conv, w_out, x, gate, l, *, tm=512):
    B, S, D = x.shape
    ka, kc = attn.shape[-1], conv.shape[-1]
    vmem = 2 * (ka + kc) * D * 2 + 2 * tm * (ka + kc) * 2 + 4 * tm * D * 4 + 2 * tm * D * 4 + (8 << 20)
    return pl.pallas_call(
        _mix_out_kernel,
        grid=(B, S // tm),
        in_specs=[pl.BlockSpec((1, tm, ka), lambda b, i: (b, i, 0)),
                  pl.BlockSpec((1, tm, kc), lambda b, i: (b, i, 0)),
                  pl.BlockSpec((1, ka + kc, D), lambda b, i: (l, 0, 0)),
                  pl.BlockSpec((1, tm, D), lambda b, i: (b, i, 0)),
                  pl.BlockSpec((1, 1, D), lambda b, i: (b, 0, 0))],
        out_specs=pl.BlockSpec((1, tm, D), lambda b, i: (b, i, 0)),
        out_shape=jax.ShapeDtypeStruct((B, S, D), F32),
        compiler_params=_params(("parallel", "parallel"), vmem),
        name="mix_out",
    )(attn, conv, w_out, x, gate)


def _ffn_kernel(x_ref, sh_ref, sc_ref, gt_ref, g_ref, wg_ref, wv_ref, dwg_ref, dwv_ref,
                dbg_ref, dbv_ref, wd_ref, fg_ref, o_ref, h_scr, ubuf, carry_ref, *, tm, tf, final_norm):
    i = pl.program_id(1)
    j = pl.program_id(2)
    pad = V7X_SUBLANES

    @pl.when(j == 0)
    def _():
        _norm_mod_rows(x_ref, g_ref, sh_ref, sc_ref, h_scr, tm)
        o_ref[...] = jnp.zeros_like(o_ref)

    h = h_scr[...]
    th = tf // FFN_SPLIT
    halves = [slice(s * th, (s + 1) * th) for s in range(FFN_SPLIT)]
    ups = [(jnp.dot(h, wg_ref[0, :, cols], preferred_element_type=F32),
            jnp.dot(h, wv_ref[0, :, cols], preferred_element_type=F32)) for cols in halves]

    def conv(idx, u, cols, dw_ref, db_ref):
        prev = carry_ref[j, idx, :, cols]
        ubuf[idx, 0:pad, cols] = jnp.where(i == 0, 0.0, prev)
        ubuf[idx, pad:, cols] = u
        carry_ref[j, idx, :, cols] = u[tm - pad:, :]
        return (dw_ref[0, 2:3, cols] * ubuf[idx, pad:pad + tm, cols]
                + dw_ref[0, 1:2, cols] * ubuf[idx, pad - 1:pad - 1 + tm, cols]
                + dw_ref[0, 0:1, cols] * ubuf[idx, pad - 2:pad - 2 + tm, cols]
                + db_ref[0, :, cols])

    for cols, (ug, uv) in zip(halves, ups):
        yg = conv(0, ug, cols, dwg_ref, dbg_ref)
        yv = conv(1, uv, cols, dwv_ref, dbv_ref)
        act = (yg * jax.nn.sigmoid(yg) * yv).astype(BF16)
        o_ref[0] += jnp.dot(act, wd_ref[0, cols, :], preferred_element_type=F32)

    @pl.when(j == pl.num_programs(2) - 1)
    def _():
        if final_norm:
            rows = 256

            def body(r, carry):
                sl = pl.ds(pl.multiple_of(r * rows, rows), rows)
                y = x_ref[0, sl, :] + gt_ref[0] * o_ref[0, sl, :]
                ms = jnp.mean(y * y, axis=-1, keepdims=True)
                o_ref[0, sl, :] = y * lax.rsqrt(ms + RMS_EPS) * fg_ref[0]
                return carry

            lax.fori_loop(0, tm // rows, body, 0)
        else:
            o_ref[0] = x_ref[0] + gt_ref[0] * o_ref[0]


def _ffn_call(x, sh, sc, gate, g, w_up, dw_w, dw_b, w_down, final_g, l, *, final_norm, tm=1024, tf=FFN_TILE):
    B, S, D = x.shape
    L, FF, _ = w_down.shape
    nf = FF // tf
    dw_b3 = dw_b.reshape(L, 1, 2 * FF)
    vmem = 3 * tm * D * 4 + tm * D * 2 + 2 * (tm + 8) * tf * 4 \
        + 2 * 2 * D * tf * 2 + 2 * tf * D * 2 + 6 * tm * tf * 4 + (4 << 20)
    return pl.pallas_call(
        functools.partial(_ffn_kernel, tm=tm, tf=tf, final_norm=final_norm),
        grid=(B, S // tm, nf),
        in_specs=[pl.BlockSpec((1, tm, D), lambda b, i, j: (b, i, 0), pipeline_mode=pl.Buffered(1)),
                  pl.BlockSpec((1, 1, D), lambda b, i, j: (b, 0, 0)),
                  pl.BlockSpec((1, 1, D), lambda b, i, j: (b, 0, 0)),
                  pl.BlockSpec((1, 1, D), lambda b, i, j: (b, 0, 0)),
                  pl.BlockSpec((1, 1, D), lambda b, i, j: (l, 0, 0)),
                  pl.BlockSpec((1, D, tf), lambda b, i, j: (l, 0, j)),
                  pl.BlockSpec((1, D, tf), lambda b, i, j: (l, 0, nf + j)),
                  pl.BlockSpec((1, FFN_WIDTH, tf), lambda b, i, j: (l, 0, j)),
                  pl.BlockSpec((1, FFN_WIDTH, tf), lambda b, i, j: (l, 0, nf + j)),
                  pl.BlockSpec((1, 1, tf), lambda b, i, j: (l, 0, j)),
                  pl.BlockSpec((1, 1, tf), lambda b, i, j: (l, 0, nf + j)),
                  pl.BlockSpec((1, tf, D), lambda b, i, j: (l, j, 0)),
                  pl.BlockSpec((1, 1, D), lambda b, i, j: (0, 0, 0))],
        out_specs=pl.BlockSpec((1, tm, D), lambda b, i, j: (b, i, 0)),
        out_shape=jax.ShapeDtypeStruct((B, S, D), F32),
        scratch_shapes=[pltpu.VMEM((tm, D), BF16),
                        pltpu.VMEM((2, tm + V7X_SUBLANES, tf), F32),
                        pltpu.VMEM((nf, 2, V7X_SUBLANES, tf), F32)],
        compiler_params=_params(("parallel", "arbitrary", "arbitrary"), vmem),
        name="conv_ffn",
    )(x, sh, sc, gate, g, w_up, w_up, dw_w, dw_w, dw_b3, dw_b3, w_down, final_g)


def kernel(x, c, ada_w, ada_b, mix_norm_g, w_in, b_forget, conf_dw_w, conf_dw_b, conf_ln_g,
           conf_ln_b, sc_dw_w, w_out, ffn_norm_g, w_up, ffn_dw_w, ffn_dw_b, w_down, final_norm_g):
    B, S, D = x.shape
    L = ada_w.shape[0]

    f0 = 3 * D_ATTN
    v0 = 2 * D_ATTN
    w_main = jnp.concatenate([w_in[..., :v0], w_in[..., f0 + N_HEADS:], w_in[..., v0:f0]], axis=-1).astype(BF16)
    w_f = jnp.pad(w_in[..., f0:f0 + N_HEADS], ((0, 0), (0, 0), (0, V7X_LANES - N_HEADS))).astype(BF16)
    col_scale = jnp.where(jnp.arange(w_main.shape[-1]) < D_ATTN, HEAD_DIM ** -0.5 * LOG2E, 1.0)
    col_scale = col_scale.astype(F32).reshape(1, -1)
    b_f = jnp.pad(b_forget, ((0, 0), (0, V7X_LANES - N_HEADS))).reshape(L, 1, V7X_LANES)
    w_out_b = w_out.astype(BF16)
    w_up_b = w_up.astype(BF16)
    w_down_b = w_down.astype(BF16)

    mix_g = mix_norm_g.reshape(L, 1, D)
    ffn_g = ffn_norm_g.reshape(L, 1, D)
    c_pad = jnp.pad(c, ((0, V7X_SUBLANES - B), (0, 0)))
    ada = _ada_call(c_pad, ada_w, ada_b)[:, :B]
    ada = ada.reshape(L, B, N_ADA, 1, D)

    for l in range(L):
        sh_m, sc_m, g_m, sh_f, sc_f, g_f = (ada[l, :, k] for k in range(N_ADA))
        proj, vt, logf = _mix_in_call(x, sh_m, sc_m, mix_g, w_main, col_scale, w_f, b_f, l)
        kbias = _cumsum_call(logf)
        attn = _attn_call(proj, kbias, vt)
        conv = _conv_call(proj, conf_dw_w, conf_dw_b, conf_ln_g, conf_ln_b, sc_dw_w, l)
        x = _mix_out_call(attn, conv, w_out_b, x, g_m, l)
        x = _ffn_call(x, sh_f, sc_f, g_f, ffn_g, w_up_b, ffn_dw_w, ffn_dw_b, w_down_b,
                      final_norm_g.reshape(1, 1, D), l, final_norm=(l == L - 1))
    return x
```

```python
import functools

import jax
import jax.numpy as jnp
from jax import lax
from jax.experimental import pallas as pl
from jax.experimental.pallas import tpu as pltpu

F32 = jnp.float32
BF16 = jnp.bfloat16

V7X_VMEM_BYTES = 64 * 1024 * 1024
V7X_LANES = 128
V7X_SUBLANES = 8

HEAD_DIM = 128
N_HEADS = 8
D_ATTN = N_HEADS * HEAD_DIM
D_CONF = 512
D_SCONV = 512
CONF_WIDTH = 31
SCONV_WIDTH = 3
FFN_WIDTH = 3
N_ADA = 6
RMS_EPS = 1e-6
LN_EPS = 1e-5
CONF_HALO = 32
SCONV_HALO = 16
FFN_TILE = 512
FFN_SPLIT = 2
LOG2E = 1.4426950408889634


def _params(semantics, vmem_bytes):
    return pltpu.CompilerParams(dimension_semantics=semantics,
                                vmem_limit_bytes=min(vmem_bytes, V7X_VMEM_BYTES - (6 << 20)))


def _ada_kernel(c_ref, w_ref, b_ref, o_ref):
    c = c_ref[...]
    ca = (c * jax.nn.sigmoid(c)).astype(BF16)
    w = w_ref[0].astype(BF16)
    o_ref[0] = jnp.dot(ca, w, preferred_element_type=F32) + b_ref[0]


def _ada_call(c_pad, ada_w, ada_b):
    L, D, N = ada_w.shape
    tn = 1024
    return pl.pallas_call(
        _ada_kernel,
        grid=(L, N // tn),
        in_specs=[pl.BlockSpec((V7X_SUBLANES, D), lambda l, j: (0, 0)),
                  pl.BlockSpec((1, D, tn), lambda l, j: (l, 0, j)),
                  pl.BlockSpec((1, 1, tn), lambda l, j: (l, 0, j))],
        out_specs=pl.BlockSpec((1, V7X_SUBLANES, tn), lambda l, j: (l, 0, j)),
        out_shape=jax.ShapeDtypeStruct((L, V7X_SUBLANES, N), F32),
        compiler_params=_params(("parallel", "arbitrary"), 32 << 20),
        name="ada",
    )(c_pad, ada_w, ada_b.reshape(L, 1, N))


def _norm_mod_rows(x_ref, g_ref, sh_ref, sc_ref, h_scr, tm, rows=256):
    gs = g_ref[0] * (1.0 + sc_ref[0])
    sh = sh_ref[0]

    def body(r, carry):
        sl = pl.ds(pl.multiple_of(r * rows, rows), rows)
        x = x_ref[0, sl, :]
        ms = jnp.mean(x * x, axis=-1, keepdims=True)
        y = x * lax.rsqrt(ms + RMS_EPS)
        h_scr[sl, :] = (y * gs + sh).astype(BF16)
        return carry

    lax.fori_loop(0, tm // rows, body, 0)


def _mix_in_kernel(x_ref, sh_ref, sc_ref, g_ref, w_ref, cs_ref, wf_ref, bf_ref,
                   proj_ref, vt_ref, logf_ref, h_scr, *, tm, tn):
    _norm_mod_rows(x_ref, g_ref, sh_ref, sc_ref, h_scr, tm)
    h = h_scr[...]
    z = jnp.dot(h, wf_ref[0], preferred_element_type=F32) + bf_ref[0]
    logf_ref[0] = jnp.minimum(z, 0.0) - jnp.log1p(jnp.exp(-jnp.abs(z)))
    per = tn // V7X_LANES
    nchunk = proj_ref.shape[1]
    for c in range(w_ref.shape[-1] // tn):
        cols = slice(c * tn, (c + 1) * tn)
        r = jnp.dot(h, w_ref[0, :, cols], preferred_element_type=F32) * cs_ref[:, cols]
        for k in range(per):
            chunk = r[:, k * V7X_LANES:(k + 1) * V7X_LANES]
            idx = c * per + k
            if idx < nchunk:
                proj_ref[0, idx] = chunk.astype(BF16)
            else:
                vt_ref[0, idx - nchunk] = chunk.T.astype(BF16)


def _mix_in_call(x, sh, sc, g, w_main, col_scale, w_f, b_f, l, *, tm=512, tn=512):
    B, S, D = x.shape
    N = w_main.shape[-1]
    nchunk = N // V7X_LANES - N_HEADS
    vmem = 2 * tm * D * 4 + tm * D * 2 + D * N * 2 + 2 * D * V7X_LANES * 2 \
        + 2 * tm * N * 2 + 2 * tm * V7X_LANES * 4 + 4 * tm * tn * 4 + (4 << 20)
    return pl.pallas_call(
        functools.partial(_mix_in_kernel, tm=tm, tn=tn),
        grid=(B, S // tm),
        in_specs=[pl.BlockSpec((1, tm, D), lambda b, i: (b, i, 0)),
                  pl.BlockSpec((1, 1, D), lambda b, i: (b, 0, 0)),
                  pl.BlockSpec((1, 1, D), lambda b, i: (b, 0, 0)),
                  pl.BlockSpec((1, 1, D), lambda b, i: (l, 0, 0)),
                  pl.BlockSpec((1, D, N), lambda b, i: (l, 0, 0), pipeline_mode=pl.Buffered(1)),
                  pl.BlockSpec((1, N), lambda b, i: (0, 0)),
                  pl.BlockSpec((1, D, V7X_LANES), lambda b, i: (l, 0, 0)),
                  pl.BlockSpec((1, 1, V7X_LANES), lambda b, i: (l, 0, 0))],
        out_specs=[pl.BlockSpec((1, nchunk, tm, V7X_LANES), lambda b, i: (b, 0, i, 0)),
                   pl.BlockSpec((1, N_HEADS, HEAD_DIM, tm), lambda b, i: (b, 0, 0, i)),
                   pl.BlockSpec((1, tm, V7X_LANES), lambda b, i: (b, i, 0))],
        out_shape=[jax.ShapeDtypeStruct((B, nchunk, S, V7X_LANES), BF16),
                   jax.ShapeDtypeStruct((B, N_HEADS, HEAD_DIM, S), BF16),
                   jax.ShapeDtypeStruct((B, S, V7X_LANES), F32)],
        scratch_shapes=[pltpu.VMEM((tm, D), BF16)],
        compiler_params=_params(("parallel", "parallel"), vmem),
        name="mix_in",
    )(x, sh, sc, g, w_main, col_scale, w_f, b_f)


def _cumsum_kernel(l_ref, o_ref, carry_ref, *, ts):
    i = pl.program_id(1)

    @pl.when(i == 0)
    def _():
        carry_ref[...] = jnp.zeros_like(carry_ref)

    lf = l_ref[0]
    row = lax.broadcasted_iota(jnp.int32, (ts, ts), 0)
    col = lax.broadcasted_iota(jnp.int32, (ts, ts), 1)
    tri = jnp.where(col <= row, 1.0, 0.0).astype(BF16)
    hi = lf.astype(BF16)
    r1 = lf - hi.astype(F32)
    mid = r1.astype(BF16)
    lo = (r1 - mid.astype(F32)).astype(BF16)
    f = (jnp.dot(tri, hi, preferred_element_type=F32)
         + jnp.dot(tri, mid, preferred_element_type=F32)
         + jnp.dot(tri, lo, preferred_element_type=F32)
         + carry_ref[0:1, :])
    carry_ref[...] = jnp.broadcast_to(f[ts - 1:ts, :], carry_ref.shape)
    nf = f * (-LOG2E)
    lane = lax.broadcasted_iota(jnp.int32, (ts, V7X_LANES), 1)
    for hd in range(N_HEADS):
        v = jnp.broadcast_to(nf[:, hd:hd + 1], (ts, V7X_LANES))
        p0 = v.astype(BF16).astype(F32)
        p1 = (v - p0).astype(BF16).astype(F32)
        p2 = (v - p0 - p1).astype(BF16).astype(F32)
        pieces = jnp.where(lane == 0, p0, jnp.where(lane == 1, p1, jnp.where(lane == 2, p2, 0.0)))
        o_ref[0, hd] = pieces.astype(BF16)


def _cumsum_call(logf, *, ts=512):
    B, S, _ = logf.shape
    return pl.pallas_call(
        functools.partial(_cumsum_kernel, ts=ts),
        grid=(B, S // ts),
        in_specs=[pl.BlockSpec((1, ts, V7X_LANES), lambda b, i: (b, i, 0))],
        out_specs=pl.BlockSpec((1, N_HEADS, ts, V7X_LANES), lambda b, i: (b, 0, i, 0)),
        out_shape=jax.ShapeDtypeStruct((B, N_HEADS, S, V7X_LANES), BF16),
        scratch_shapes=[pltpu.VMEM((V7X_SUBLANES, V7X_LANES), F32)],
        compiler_params=_params(("parallel", "arbitrary"), 32 << 20),
        name="forget_cumsum",
    )(logf)


N_BIAS_LANES = 3
EXP2_ZERO = -150.0
NORM_SLACK = 1.001


def _attn_kernel(q_ref, k_ref, kb_ref, vt_ref, o_ref, s0, s1, m_scr, l_scr, acc_scr, k2_scr, *, tk, tg, nsub):
    i = pl.program_id(2)
    slots = (s0, s1)
    lane = lax.broadcasted_iota(jnp.int32, (tg, V7X_LANES), 1)
    ones = jnp.where(lane < N_BIAS_LANES, 1.0, 0.0).astype(BF16)
    qa = [jnp.concatenate([q_ref[0, 0, a * tg:(a + 1) * tg, :], ones], axis=1) for a in range(nsub)]

    def keys_of(a, diag):
        return (a + 1) * tg if diag else tk

    def scores(tile, slot, diag=False):
        start = pl.multiple_of(tile * tk, tk)
        ka = jnp.concatenate([k_ref[0, 0, pl.ds(start, tk), :], kb_ref[0, 0, pl.ds(start, tk), :]], axis=1)
        for a in range(nsub):
            n = keys_of(a, diag)
            slots[slot][a, 0:n, :] = lax.dot_general(ka[0:n], qa[a], (((1,), (1,)), ((), ())),
                                                     preferred_element_type=F32)

    def softmax_pv(tile, slot, diag=False):
        start = pl.multiple_of(tile * tk, tk)
        ps, alphas = [], []
        for a in range(nsub):
            n = keys_of(a, diag)
            st = slots[slot][a, 0:n, :]
            if diag:
                key = lax.broadcasted_iota(jnp.int32, (n, tg), 0)
                qry = lax.broadcasted_iota(jnp.int32, (n, tg), 1)
                st = jnp.where(key - a * tg <= qry, st, -jnp.inf)
            m = m_scr[a, 0:1, :]
            m_new = jnp.maximum(m, jnp.max(st, axis=0, keepdims=True))
            p = jnp.exp2(st - m_new)
            alpha = jnp.exp2(m - m_new)
            m_scr[a, 0:1, :] = m_new
            l_scr[a, 0:1, :] = alpha * l_scr[a, 0:1, :] + jnp.sum(p, axis=0, keepdims=True)
            ps.append(p.astype(BF16))
            alphas.append(alpha)
        for a in range(nsub):
            n = keys_of(a, diag)
            vt = vt_ref[0, 0, :, pl.ds(start, n)]
            acc_scr[a] = alphas[a] * acc_scr[a] + jnp.dot(vt, ps[a], preferred_element_type=F32)

    m_scr[...] = jnp.full(m_scr.shape, -jnp.inf, F32)
    l_scr[...] = jnp.zeros_like(l_scr)
    acc_scr[...] = jnp.zeros_like(acc_scr)

    @pl.when(i == 0)
    def _():
        def kmax(r, best):
            kf = k_ref[0, 0, pl.ds(pl.multiple_of(r * tk, tk), tk), :].astype(F32)
            return jnp.maximum(best, jnp.max(jnp.sum(kf * kf, axis=-1, keepdims=True), axis=0, keepdims=True))
        k2_scr[...] = jnp.broadcast_to(lax.fori_loop(0, k_ref.shape[2] // tk, kmax, jnp.zeros((1, 1), F32)),
                                       k2_scr.shape)

    k2 = k2_scr[0:1, 0:1]
    ubs = []
    for a in range(nsub):
        qf = q_ref[0, 0, a * tg:(a + 1) * tg, :].astype(F32)
        q2 = jnp.max(jnp.sum(qf * qf, axis=-1, keepdims=True), axis=0, keepdims=True)
        ubs.append(jnp.sqrt(q2 * k2) * NORM_SLACK)

    def negligible(tile):
        last = kb_ref[0, 0, pl.ds(pl.multiple_of((tile + 1) * tk - V7X_SUBLANES, V7X_SUBLANES), V7X_SUBLANES), :]
        nf_last = jnp.sum(last[V7X_SUBLANES - 1:, :].astype(F32), axis=-1, keepdims=True)
        worst = ubs[0] - m_scr[0, 0:1, :]
        for a in range(1, nsub):
            worst = jnp.maximum(worst, ubs[a] - m_scr[a, 0:1, :])
        worst = jnp.max(worst, axis=-1, keepdims=True) + nf_last
        return (worst[0, 0] < EXP2_ZERO).astype(jnp.int32)

    scores(i, 0, diag=True)

    @pl.when(i >= 1)
    def _():
        scores(i - 1, 1)

    softmax_pv(i, 0, diag=True)

    n_pairs = jnp.maximum(i - 1, 0) // 2

    def check(tile):
        return jnp.where(tile >= 0, negligible(jnp.maximum(tile, 0)), 1)

    def pair(carry):
        jj, _ = carry
        n = 2 * jj + 1
        scores(i - n - 1, 0)
        softmax_pv(i - n, 1)
        scores(i - n - 2, 1)
        softmax_pv(i - n - 1, 0)
        return jj + 1, check(i - n - 2)

    _, done = lax.while_loop(lambda c: jnp.logical_and(c[0] < n_pairs, c[1] == 0), pair,
                             (jnp.int32(0), check(i - 1)))
    more = done == 0

    @pl.when(jnp.logical_and(more, i % 2 == 1))
    def _():
        softmax_pv(0, 1)

    @pl.when(jnp.logical_and(more, jnp.logical_and(i % 2 == 0, i >= 2)))
    def _():
        scores(0, 0)
        softmax_pv(1, 1)
        softmax_pv(0, 0)

    for a in range(nsub):
        o_ref[0, a * tg:(a + 1) * tg, :] = (acc_scr[a] / l_scr[a, 0:1, :]).T.astype(BF16)


def _attn_call(proj, kbias, vt, *, tk=1024, tg=256):
    B, _, S, _ = proj.shape
    nsub = tk // tg
    return pl.pallas_call(
        functools.partial(_attn_kernel, tk=tk, tg=tg, nsub=nsub),
        grid=(B, N_HEADS, S // tk),
        in_specs=[pl.BlockSpec((1, 1, tk, HEAD_DIM), lambda b, h, i: (b, h, i, 0)),
                  pl.BlockSpec((1, 1, S, HEAD_DIM), lambda b, h, i: (b, N_HEADS + h, 0, 0)),
                  pl.BlockSpec((1, 1, S, V7X_LANES), lambda b, h, i: (b, h, 0, 0)),
                  pl.BlockSpec((1, 1, HEAD_DIM, S), lambda b, h, i: (b, h, 0, 0))],
        out_specs=pl.BlockSpec((1, tk, HEAD_DIM), lambda b, h, i: (b, i, h)),
        out_shape=jax.ShapeDtypeStruct((B, S, D_ATTN), BF16),
        scratch_shapes=[pltpu.VMEM((nsub, tk, tg), F32),
                        pltpu.VMEM((nsub, tk, tg), F32),
                        pltpu.VMEM((nsub, V7X_SUBLANES, tg), F32),
                        pltpu.VMEM((nsub, V7X_SUBLANES, tg), F32),
                        pltpu.VMEM((nsub, HEAD_DIM, tg), F32),
                        pltpu.VMEM((V7X_SUBLANES, V7X_LANES), F32)],
        compiler_params=_params(("parallel", "parallel", "arbitrary"), 48 << 20),
        name="fox_attn",
    )(proj, proj, kbias, vt)


def _conv_kernel(cv_ref, cg_ref, cvh_ref, cgh_ref, sx_ref, sb_ref, sc_ref, sxh_ref, sch_ref,
                 cw_ref, cb_ref, lg_ref, lb_ref, sw_ref, o_ref, pbuf, sbuf, *, ts, rows):
    first = pl.program_id(1) == 0
    nl = D_CONF // V7X_LANES
    cbuf = pbuf.at[0]
    for c in range(nl):
        cols = slice(c * V7X_LANES, (c + 1) * V7X_LANES)
        gh = cvh_ref[0, c].astype(F32) * jax.nn.sigmoid(cgh_ref[0, c].astype(F32))
        cbuf[0:CONF_HALO, cols] = jnp.where(first, 0.0, gh)
        cbuf[CONF_HALO:, cols] = cv_ref[0, c].astype(F32) * jax.nn.sigmoid(cg_ref[0, c].astype(F32))
        zh = sch_ref[0, c].astype(F32) * sxh_ref[0, c].astype(F32)
        sbuf[0:SCONV_HALO, cols] = jnp.where(first, 0.0, zh)
        sbuf[SCONV_HALO:, cols] = sc_ref[0, c].astype(F32) * sx_ref[0, c].astype(F32)

    span = ts + CONF_HALO - V7X_SUBLANES
    for s in range(1, V7X_SUBLANES):
        pbuf[s, 0:span, :] = pbuf[0, s:s + span, :]

    cb = cb_ref[0]
    lg = lg_ref[0]
    lb = lb_ref[0]
    for r in range(ts // rows):
        r0 = r * rows
        acc = jnp.zeros((rows, D_CONF), F32)
        for k in range(CONF_WIDTH):
            o = CONF_HALO - (CONF_WIDTH - 1) + k
            ph, al = o % V7X_SUBLANES, r0 + o - o % V7X_SUBLANES
            acc = acc + cw_ref[0, k:k + 1, :] * pbuf[ph, al:al + rows, :]
        acc = acc + cb
        mu = jnp.mean(acc, axis=-1, keepdims=True)
        xc = acc - mu
        var = jnp.mean(xc * xc, axis=-1, keepdims=True)
        y = xc * lax.rsqrt(var + LN_EPS) * lg + lb
        o_ref[0, r0:r0 + rows, 0:D_CONF] = (y * jax.nn.sigmoid(y)).astype(BF16)

        sacc = jnp.zeros((rows, D_SCONV), F32)
        for k in range(SCONV_WIDTH):
            o = r0 + SCONV_HALO - (SCONV_WIDTH - 1) + k
            sacc = sacc + sw_ref[0, k:k + 1, :] * sbuf[o:o + rows, :]
        for c in range(nl):
            cols = slice(c * V7X_LANES, (c + 1) * V7X_LANES)
            gated = sb_ref[0, c, r0:r0 + rows, :].astype(F32) * sacc[:, cols]
            o_ref[0, r0:r0 + rows, D_CONF + c * V7X_LANES:D_CONF + (c + 1) * V7X_LANES] = gated.astype(BF16)


def _conv_call(proj, conf_dw_w, conf_dw_b, conf_ln_g, conf_ln_b, sc_dw_w, l, *, ts=256, rows=32):
    B, _, S, _ = proj.shape
    L = conf_dw_w.shape[0]
    nl = D_CONF // V7X_LANES
    base = 2 * D_ATTN // V7X_LANES // nl

    def tile(g):
        return pl.BlockSpec((1, nl, ts, V7X_LANES), lambda b, i: (b, base + g, i, 0))

    def halo(g, n):
        return pl.BlockSpec((1, nl, n, V7X_LANES),
                            lambda b, i: (b, base + g, jnp.maximum(i * (ts // n) - 1, 0), 0))

    def wspec(k):
        return pl.BlockSpec((1, k, D_CONF), lambda b, i: (l, 0, 0))

    return pl.pallas_call(
        functools.partial(_conv_kernel, ts=ts, rows=rows),
        grid=(B, S // ts),
        in_specs=[tile(0), tile(1), halo(0, CONF_HALO), halo(1, CONF_HALO),
                  tile(2), tile(3), tile(4), halo(2, SCONV_HALO), halo(4, SCONV_HALO),
                  wspec(CONF_WIDTH), wspec(1), wspec(1), wspec(1), wspec(SCONV_WIDTH)],
        out_specs=pl.BlockSpec((1, ts, D_CONF + D_SCONV), lambda b, i: (b, i, 0)),
        out_shape=jax.ShapeDtypeStruct((B, S, D_CONF + D_SCONV), BF16),
        scratch_shapes=[pltpu.VMEM((V7X_SUBLANES, ts + CONF_HALO, D_CONF), F32),
                        pltpu.VMEM((ts + SCONV_HALO, D_SCONV), F32)],
        compiler_params=_params(("parallel", "arbitrary"), 32 << 20),
        name="conv_mixers",
    )(proj, proj, proj, proj, proj, proj, proj, proj, proj,
      conf_dw_w, conf_dw_b.reshape(L, 1, D_CONF), conf_ln_g.reshape(L, 1, D_CONF),
      conf_ln_b.reshape(L, 1, D_CONF), sc_dw_w)


def _mix_out_kernel(a_ref, c_ref, w_ref, x_ref, g_ref, o_ref):
    ka = a_ref.shape[-1]
    r = jnp.dot(a_ref[0], w_ref[0, 0:ka, :], preferred_element_type=F32)
    r = r + jnp.dot(c_ref[0], w_ref[0, ka:, :], preferred_element_type=F32)
    o_ref[0] = x_ref[0] + g_ref[0] * r


def _mix_out_call(attn, conv, w_out, x, gate, l, *, tm=512):
    B, S, D = x.shape
    ka, kc = attn.shape[-1], conv.shape[-1]
    vmem = 2 * (ka + kc) * D * 2 + 2 * tm * (ka + kc) * 2 + 4 * tm * D * 4 + 2 * tm * D * 4 + (8 << 20)
    return pl.pallas_call(
        _mix_out_kernel,
        grid=(B, S // tm),
        in_specs=[pl.BlockSpec((1, tm, ka), lambda b, i: (b, i, 0)),
                  pl.BlockSpec((1, tm, kc), lambda b, i: (b, i, 0)),
                  pl.BlockSpec((1, ka + kc, D), lambda b, i: (l, 0, 0)),
                  pl.BlockSpec((1, tm, D), lambda b, i: (b, i, 0)),
                  pl.BlockSpec((1, 1, D), lambda b, i: (b, 0, 0))],
        out_specs=pl.BlockSpec((1, tm, D), lambda b, i: (b, i, 0)),
        out_shape=jax.ShapeDtypeStruct((B, S, D), F32),
        compiler_params=_params(("parallel", "parallel"), vmem),
        name="mix_out",
    )(attn, conv, w_out, x, gate)


def _ffn_kernel(x_ref, sh_ref, sc_ref, gt_ref, g_ref, wg_ref, wv_ref, dwg_ref, dwv_ref,
                dbg_ref, dbv_ref, wd_ref, fg_ref, o_ref, h_scr, ubuf, carry_ref, *, tm, tf, final_norm):
    i = pl.program_id(1)
    j = pl.program_id(2)
    pad = V7X_SUBLANES

    @pl.when(j == 0)
    def _():
        _norm_mod_rows(x_ref, g_ref, sh_ref, sc_ref, h_scr, tm)
        o_ref[...] = jnp.zeros_like(o_ref)

    h = h_scr[...]
    th = tf // FFN_SPLIT
    halves = [slice(s * th, (s + 1) * th) for s in range(FFN_SPLIT)]
    ups = [(jnp.dot(h, wg_ref[0, :, cols], preferred_element_type=F32),
            jnp.dot(h, wv_ref[0, :, cols], preferred_element_type=F32)) for cols in halves]

    def conv(idx, u, cols, dw_ref, db_ref):
        prev = carry_ref[j, idx, :, cols]
        ubuf[idx, 0:pad, cols] = jnp.where(i == 0, 0.0, prev)
        ubuf[idx, pad:, cols] = u
        carry_ref[j, idx, :, cols] = u[tm - pad:, :]
        return (dw_ref[0, 2:3, cols] * ubuf[idx, pad:pad + tm, cols]
                + dw_ref[0, 1:2, cols] * ubuf[idx, pad - 1:pad - 1 + tm, cols]
                + dw_ref[0, 0:1, cols] * ubuf[idx, pad - 2:pad - 2 + tm, cols]
                + db_ref[0, :, cols])

    for cols, (ug, uv) in zip(halves, ups):
        yg = conv(0, ug, cols, dwg_ref, dbg_ref)
        yv = conv(1, uv, cols, dwv_ref, dbv_ref)
        act = (yg * jax.nn.sigmoid(yg) * yv).astype(BF16)
        o_ref[0] += jnp.dot(act, wd_ref[0, cols, :], preferred_element_type=F32)

    @pl.when(j == pl.num_programs(2) - 1)
    def _():
        if final_norm:
            rows = 256

            def body(r, carry):
                sl = pl.ds(pl.multiple_of(r * rows, rows), rows)
                y = x_ref[0, sl, :] + gt_ref[0] * o_ref[0, sl, :]
                ms = jnp.mean(y * y, axis=-1, keepdims=True)
                o_ref[0, sl, :] = y * lax.rsqrt(ms + RMS_EPS) * fg_ref[0]
                return carry

            lax.fori_loop(0, tm // rows, body, 0)
        else:
            o_ref[0] = x_ref[0] + gt_ref[0] * o_ref[0]


def _ffn_call(x, sh, sc, gate, g, w_up, dw_w, dw_b, w_down, final_g, l, *, final_norm, tm=1024, tf=FFN_TILE):
    B, S, D = x.shape
    L, FF, _ = w_down.shape
    nf = FF // tf
    dw_b3 = dw_b.reshape(L, 1, 2 * FF)
    vmem = 3 * tm * D * 4 + tm * D * 2 + 2 * (tm + 8) * tf * 4 \
        + 2 * 2 * D * tf * 2 + 2 * tf * D * 2 + 6 * tm * tf * 4 + (4 << 20)
    return pl.pallas_call(
        functools.partial(_ffn_kernel, tm=tm, tf=tf, final_norm=final_norm),
        grid=(B, S // tm, nf),
        in_specs=[pl.BlockSpec((1, tm, D), lambda b, i, j: (b, i, 0), pipeline_mode=pl.Buffered(1)),
                  pl.BlockSpec((1, 1, D), lambda b, i, j: (b, 0, 0)),
                  pl.BlockSpec((1, 1, D), lambda b, i, j: (b, 0, 0)),
                  pl.BlockSpec((1, 1, D), lambda b, i, j: (b, 0, 0)),
                  pl.BlockSpec((1, 1, D), lambda b, i, j: (l, 0, 0)),
                  pl.BlockSpec((1, D, tf), lambda b, i, j: (l, 0, j)),
                  pl.BlockSpec((1, D, tf), lambda b, i, j: (l, 0, nf + j)),
                  pl.BlockSpec((1, FFN_WIDTH, tf), lambda b, i, j: (l, 0, j)),
                  pl.BlockSpec((1, FFN_WIDTH, tf), lambda b, i, j: (l, 0, nf + j)),
                  pl.BlockSpec((1, 1, tf), lambda b, i, j: (l, 0, j)),
                  pl.BlockSpec((1, 1, tf), lambda b, i, j: (l, 0, nf + j)),
                  pl.BlockSpec((1, tf, D), lambda b, i, j: (l, j, 0)),
                  pl.BlockSpec((1, 1, D), lambda b, i, j: (0, 0, 0))],
        out_specs=pl.BlockSpec((1, tm, D), lambda b, i, j: (b, i, 0)),
        out_shape=jax.ShapeDtypeStruct((B, S, D), F32),
        scratch_shapes=[pltpu.VMEM((tm, D), BF16),
                        pltpu.VMEM((2, tm + V7X_SUBLANES, tf), F32),
                        pltpu.VMEM((nf, 2, V7X_SUBLANES, tf), F32)],
        compiler_params=_params(("parallel", "arbitrary", "arbitrary"), vmem),
        name="conv_ffn",
    )(x, sh, sc, gate, g, w_up, w_up, dw_w, dw_w, dw_b3, dw_b3, w_down, final_g)


def kernel(x, c, ada_w, ada_b, mix_norm_g, w_in, b_forget, conf_dw_w, conf_dw_b, conf_ln_g,
           conf_ln_b, sc_dw_w, w_out, ffn_norm_g, w_up, ffn_dw_w, ffn_dw_b, w_down, final_norm_g):
    B, S, D = x.shape
    L = ada_w.shape[0]

    f0 = 3 * D_ATTN
    v0 = 2 * D_ATTN
    w_main = jnp.concatenate([w_in[..., :v0], w_in[..., f0 + N_HEADS:], w_in[..., v0:f0]], axis=-1).astype(BF16)
    w_f = jnp.pad(w_in[..., f0:f0 + N_HEADS], ((0, 0), (0, 0), (0, V7X_LANES - N_HEADS))).astype(BF16)
    col_scale = jnp.where(jnp.arange(w_main.shape[-1]) < D_ATTN, HEAD_DIM ** -0.5 * LOG2E, 1.0)
    col_scale = col_scale.astype(F32).reshape(1, -1)
    b_f = jnp.pad(b_forget, ((0, 0), (0, V7X_LANES - N_HEADS))).reshape(L, 1, V7X_LANES)
    w_out_b = w_out.astype(BF16)
    w_up_b = w_up.astype(BF16)
    w_down_b = w_down.astype(BF16)

    mix_g = mix_norm_g.reshape(L, 1, D)
    ffn_g = ffn_norm_g.reshape(L, 1, D)
    c_pad = jnp.pad(c, ((0, V7X_SUBLANES - B), (0, 0)))
    ada = _ada_call(c_pad, ada_w, ada_b)[:, :B]
    ada = ada.reshape(L, B, N_ADA, 1, D)

    for l in range(L):
        sh_m, sc_m, g_m, sh_f, sc_f, g_f = (ada[l, :, k] for k in range(N_ADA))
        proj, vt, logf = _mix_in_call(x, sh_m, sc_m, mix_g, w_main, col_scale, w_f, b_f, l)
        kbias = _cumsum_call(logf)
        attn = _attn_call(proj, kbias, vt)
        conv = _conv_call(proj, conf_dw_w, conf_dw_b, conf_ln_g, conf_ln_b, sc_dw_w, l)
        x = _mix_out_call(attn, conv, w_out_b, x, g_m, l)
        x = _ffn_call(x, sh_f, sc_f, g_f, ffn_g, w_up_b, ffn_dw_w, ffn_dw_b, w_down_b,
                      final_norm_g.reshape(1, 1, D), l, final_norm=(l == L - 1))
    return x
```

```python
import functools

import jax
import jax.numpy as jnp
from jax import lax
from jax.experimental import pallas as pl
from jax.experimental.pallas import tpu as pltpu

F32 = jnp.float32
BF16 = jnp.bfloat16

V7X_VMEM_BYTES = 64 * 1024 * 1024
V7X_LANES = 128
V7X_SUBLANES = 8

HEAD_DIM = 128
N_HEADS = 8
D_ATTN = N_HEADS * HEAD_DIM
D_CONF = 512
D_SCONV = 512
CONF_WIDTH = 31
SCONV_WIDTH = 3
FFN_WIDTH = 3
N_ADA = 6
RMS_EPS = 1e-6
LN_EPS = 1e-5
CONF_HALO = 32
SCONV_HALO = 16
FFN_TILE = 512
FFN_SPLIT = 2
LOG2E = 1.4426950408889634


def _params(semantics, vmem_bytes):
    return pltpu.CompilerParams(dimension_semantics=semantics,
                                vmem_limit_bytes=min(vmem_bytes, V7X_VMEM_BYTES - (6 << 20)))


def _ada_kernel(c_ref, w_ref, b_ref, o_ref):
    c = c_ref[...]
    ca = (c * jax.nn.sigmoid(c)).astype(BF16)
    w = w_ref[0].astype(BF16)
    o_ref[0] = jnp.dot(ca, w, preferred_element_type=F32) + b_ref[0]


def _ada_call(c_pad, ada_w, ada_b):
    L, D, N = ada_w.shape
    tn = 1024
    return pl.pallas_call(
        _ada_kernel,
        grid=(L, N // tn),
        in_specs=[pl.BlockSpec((V7X_SUBLANES, D), lambda l, j: (0, 0)),
                  pl.BlockSpec((1, D, tn), lambda l, j: (l, 0, j)),
                  pl.BlockSpec((1, 1, tn), lambda l, j: (l, 0, j))],
        out_specs=pl.BlockSpec((1, V7X_SUBLANES, tn), lambda l, j: (l, 0, j)),
        out_shape=jax.ShapeDtypeStruct((L, V7X_SUBLANES, N), F32),
        compiler_params=_params(("parallel", "arbitrary"), 32 << 20),
        name="ada",
    )(c_pad, ada_w, ada_b.reshape(L, 1, N))


def _norm_mod_rows(x_ref, g_ref, sh_ref, sc_ref, h_scr, tm, rows=256):
    gs = g_ref[0] * (1.0 + sc_ref[0])
    sh = sh_ref[0]

    def body(r, carry):
        sl = pl.ds(pl.multiple_of(r * rows, rows), rows)
        x = x_ref[0, sl, :]
        ms = jnp.mean(x * x, axis=-1, keepdims=True)
        y = x * lax.rsqrt(ms + RMS_EPS)
        h_scr[sl, :] = (y * gs + sh).astype(BF16)
        return carry

    lax.fori_loop(0, tm // rows, body, 0)


def _mix_in_kernel(x_ref, sh_ref, sc_ref, g_ref, w_ref, cs_ref, wf_ref, bf_ref,
                   proj_ref, vt_ref, logf_ref, h_scr, *, tm, tn):
    _norm_mod_rows(x_ref, g_ref, sh_ref, sc_ref, h_scr, tm)
    h = h_scr[...]
    z = jnp.dot(h, wf_ref[0], preferred_element_type=F32) + bf_ref[0]
    logf_ref[0] = jnp.minimum(z, 0.0) - jnp.log1p(jnp.exp(-jnp.abs(z)))
    per = tn // V7X_LANES
    nchunk = proj_ref.shape[1]
    for c in range(w_ref.shape[-1] // tn):
        cols = slice(c * tn, (c + 1) * tn)
        r = jnp.dot(h, w_ref[0, :, cols], preferred_element_type=F32) * cs_ref[:, cols]
        for k in range(per):
            chunk = r[:, k * V7X_LANES:(k + 1) * V7X_LANES]
            idx = c * per + k
            if idx < nchunk:
                proj_ref[0, idx] = chunk.astype(BF16)
            else:
                vt_ref[0, idx - nchunk] = chunk.T.astype(BF16)


def _mix_in_call(x, sh, sc, g, w_main, col_scale, w_f, b_f, l, *, tm=512, tn=512):
    B, S, D = x.shape
    N = w_main.shape[-1]
    nchunk = N // V7X_LANES - N_HEADS
    vmem = 2 * tm * D * 4 + tm * D * 2 + D * N * 2 + 2 * D * V7X_LANES * 2 \
        + 2 * tm * N * 2 + 2 * tm * V7X_LANES * 4 + 4 * tm * tn * 4 + (4 << 20)
    return pl.pallas_call(
        functools.partial(_mix_in_kernel, tm=tm, tn=tn),
        grid=(B, S // tm),
        in_specs=[pl.BlockSpec((1, tm, D), lambda b, i: (b, i, 0)),
                  pl.BlockSpec((1, 1, D), lambda b, i: (b, 0, 0)),
                  pl.BlockSpec((1, 1, D), lambda b, i: (b, 0, 0)),
                  pl.BlockSpec((1, 1, D), lambda b, i: (l, 0, 0)),
                  pl.BlockSpec((1, D, N), lambda b, i: (l, 0, 0), pipeline_mode=pl.Buffered(1)),
                  pl.BlockSpec((1, N), lambda b, i: (0, 0)),
                  pl.BlockSpec((1, D, V7X_LANES), lambda b, i: (l, 0, 0)),
                  pl.BlockSpec((1, 1, V7X_LANES), lambda b, i: (l, 0, 0))],
        out_specs=[pl.BlockSpec((1, nchunk, tm, V7X_LANES), lambda b, i: (b, 0, i, 0)),
                   pl.BlockSpec((1, N_HEADS, HEAD_DIM, tm), lambda b, i: (b, 0, 0, i)),
                   pl.BlockSpec((1, tm, V7X_LANES), lambda b, i: (b, i, 0))],
        out_shape=[jax.ShapeDtypeStruct((B, nchunk, S, V7X_LANES), BF16),
                   jax.ShapeDtypeStruct((B, N_HEADS, HEAD_DIM, S), BF16),
                   jax.ShapeDtypeStruct((B, S, V7X_LANES), F32)],
        scratch_shapes=[pltpu.VMEM((tm, D), BF16)],
        compiler_params=_params(("parallel", "parallel"), vmem),
        name="mix_in",
    )(x, sh, sc, g, w_main, col_scale, w_f, b_f)


def _cumsum_kernel(l_ref, o_ref, carry_ref, *, ts):
    i = pl.program_id(1)

    @pl.when(i == 0)
    def _():
        carry_ref[...] = jnp.zeros_like(carry_ref)

    lf = l_ref[0]
    row = lax.broadcasted_iota(jnp.int32, (ts, ts), 0)
    col = lax.broadcasted_iota(jnp.int32, (ts, ts), 1)
    tri = jnp.where(col <= row, 1.0, 0.0).astype(BF16)
    hi = lf.astype(BF16)
    r1 = lf - hi.astype(F32)
    mid = r1.astype(BF16)
    lo = (r1 - mid.astype(F32)).astype(BF16)
    f = (jnp.dot(tri, hi, preferred_element_type=F32)
         + jnp.dot(tri, mid, preferred_element_type=F32)
         + jnp.dot(tri, lo, preferred_element_type=F32)
         + carry_ref[0:1, :])
    carry_ref[...] = jnp.broadcast_to(f[ts - 1:ts, :], carry_ref.shape)
    nf = f * (-LOG2E)
    lane = lax.broadcasted_iota(jnp.int32, (ts, V7X_LANES), 1)
    for hd in range(N_HEADS):
        v = jnp.broadcast_to(nf[:, hd:hd + 1], (ts, V7X_LANES))
        p0 = v.astype(BF16).astype(F32)
        p1 = (v - p0).astype(BF16).astype(F32)
        p2 = (v - p0 - p1).astype(BF16).astype(F32)
        pieces = jnp.where(lane == 0, p0, jnp.where(lane == 1, p1, jnp.where(lane == 2, p2, 0.0)))
        o_ref[0, hd] = pieces.astype(BF16)


def _cumsum_call(logf, *, ts=512):
    B, S, _ = logf.shape
    return pl.pallas_call(
        functools.partial(_cumsum_kernel, ts=ts),
        grid=(B, S // ts),
        in_specs=[pl.BlockSpec((1, ts, V7X_LANES), lambda b, i: (b, i, 0))],
        out_specs=pl.BlockSpec((1, N_HEADS, ts, V7X_LANES), lambda b, i: (b, 0, i, 0)),
        out_shape=jax.ShapeDtypeStruct((B, N_HEADS, S, V7X_LANES), BF16),
        scratch_shapes=[pltpu.VMEM((V7X_SUBLANES, V7X_LANES), F32)],
        compiler_params=_params(("parallel", "arbitrary"), 32 << 20),
        name="forget_cumsum",
    )(logf)


N_BIAS_LANES = 3
EXP2_ZERO = -150.0
NORM_SLACK = 1.001


def _attn_kernel(q_ref, k_ref, kb_ref, vt_ref, o_ref, s0, s1, m_scr, l_scr, acc_scr, k2_scr, *, tk, tg, nsub):
    i = pl.program_id(2)
    slots = (s0, s1)
    lane = lax.broadcasted_iota(jnp.int32, (tg, V7X_LANES), 1)
    ones = jnp.where(lane < N_BIAS_LANES, 1.0, 0.0).astype(BF16)
    qa = [jnp.concatenate([q_ref[0, 0, a * tg:(a + 1) * tg, :], ones], axis=1) for a in range(nsub)]

    def keys_of(a, diag):
        return (a + 1) * tg if diag else tk

    def scores(tile, slot, diag=False):
        start = pl.multiple_of(tile * tk, tk)
        ka = jnp.concatenate([k_ref[0, 0, pl.ds(start, tk), :], kb_ref[0, 0, pl.ds(start, tk), :]], axis=1)
        for a in range(nsub):
            n = keys_of(a, diag)
            slots[slot][a, 0:n, :] = lax.dot_general(ka[0:n], qa[a], (((1,), (1,)), ((), ())),
                                                     preferred_element_type=F32)

    def softmax_pv(tile, slot, diag=False):
        start = pl.multiple_of(tile * tk, tk)
        ps, alphas = [], []
        for a in range(nsub):
            n = keys_of(a, diag)
            st = slots[slot][a, 0:n, :]
            if diag:
                key = lax.broadcasted_iota(jnp.int32, (n, tg), 0)
                qry = lax.broadcasted_iota(jnp.int32, (n, tg), 1)
                st = jnp.where(key - a * tg <= qry, st, -jnp.inf)
            m = m_scr[a, 0:1, :]
            m_new = jnp.maximum(m, jnp.max(st, axis=0, keepdims=True))
            p = jnp.exp2(st - m_new)
            alpha = jnp.exp2(m - m_new)
            m_scr[a, 0:1, :] = m_new
            l_scr[a, 0:1, :] = alpha * l_scr[a, 0:1, :] + jnp.sum(p, axis=0, keepdims=True)
            ps.append(p.astype(BF16))
            alphas.append(alpha)
        for a in range(nsub):
            n = keys_of(a, diag)
            vt = vt_ref[0, 0, :, pl.ds(start, n)]
            acc_scr[a] = alphas[a] * acc_scr[a] + jnp.dot(vt, ps[a], preferred_element_type=F32)

    m_scr[...] = jnp.full(m_scr.shape, -jnp.inf, F32)
    l_scr[...] = jnp.zeros_like(l_scr)
    acc_scr[...] = jnp.zeros_like(acc_scr)

    @pl.when(i == 0)
    def _():
        def kmax(r, best):
            kf = k_ref[0, 0, pl.ds(pl.multiple_of(r * tk, tk), tk), :].astype(F32)
            return jnp.maximum(best, jnp.max(jnp.sum(kf * kf, axis=-1, keepdims=True), axis=0, keepdims=True))
        k2_scr[...] = jnp.broadcast_to(lax.fori_loop(0, k_ref.shape[2] // tk, kmax, jnp.zeros((1, 1), F32)),
                                       k2_scr.shape)

    k2 = k2_scr[0:1, 0:1]
    ubs = []
    for a in range(nsub):
        qf = q_ref[0, 0, a * tg:(a + 1) * tg, :].astype(F32)
        q2 = jnp.max(jnp.sum(qf * qf, axis=-1, keepdims=True), axis=0, keepdims=True)
        ubs.append(jnp.sqrt(q2 * k2) * NORM_SLACK)

    def negligible(tile):
        last = kb_ref[0, 0, pl.ds(pl.multiple_of((tile + 1) * tk - V7X_SUBLANES, V7X_SUBLANES), V7X_SUBLANES), :]
        nf_last = jnp.sum(last[V7X_SUBLANES - 1:, :].astype(F32), axis=-1, keepdims=True)
        worst = ubs[0] - m_scr[0, 0:1, :]
        for a in range(1, nsub):
            worst = jnp.maximum(worst, ubs[a] - m_scr[a, 0:1, :])
        worst = jnp.max(worst, axis=-1, keepdims=True) + nf_last
        return (worst[0, 0] < EXP2_ZERO).astype(jnp.int32)

    def check(tile):
        return jnp.where(tile >= 0, negligible(jnp.maximum(tile, 0)), 1)

    scores(i, 0, diag=True)
    scores(jnp.maximum(i - 1, 0), 1)
    softmax_pv(i, 0, diag=True)
    done = check(i - 1)

    @pl.when(done == 0)
    def _():
        scores(jnp.maximum(i - 2, 0), 0)
        softmax_pv(i - 1, 1)

    done = jnp.where(done == 0, check(i - 2), 1)
    left = i - 1
    n_pairs = jnp.maximum(left - 1, 0) // 2

    def pair(carry):
        jj, _ = carry
        t = i - 2 - 2 * jj
        scores(t - 1, 1)
        softmax_pv(t, 0)
        scores(t - 2, 0)
        softmax_pv(t - 1, 1)
        return jj + 1, check(t - 2)

    _, done = lax.while_loop(lambda c: jnp.logical_and(c[0] < n_pairs, c[1] == 0), pair,
                             (jnp.int32(0), done))
    more = done == 0

    @pl.when(jnp.logical_and(more, left % 2 == 1))
    def _():
        softmax_pv(0, 0)

    @pl.when(jnp.logical_and(more, jnp.logical_and(left % 2 == 0, left >= 2)))
    def _():
        scores(0, 1)
        softmax_pv(1, 0)
        softmax_pv(0, 1)

    for a in range(nsub):
        o_ref[0, a * tg:(a + 1) * tg, :] = (acc_scr[a] / l_scr[a, 0:1, :]).T.astype(BF16)


def _attn_call(proj, kbias, vt, *, tk=1024, tg=256):
    B, _, S, _ = proj.shape
    nsub = tk // tg
    return pl.pallas_call(
        functools.partial(_attn_kernel, tk=tk, tg=tg, nsub=nsub),
        grid=(B, N_HEADS, S // tk),
        in_specs=[pl.BlockSpec((1, 1, tk, HEAD_DIM), lambda b, h, i: (b, h, i, 0)),
                  pl.BlockSpec((1, 1, S, HEAD_DIM), lambda b, h, i: (b, N_HEADS + h, 0, 0)),
                  pl.BlockSpec((1, 1, S, V7X_LANES), lambda b, h, i: (b, h, 0, 0)),
                  pl.BlockSpec((1, 1, HEAD_DIM, S), lambda b, h, i: (b, h, 0, 0))],
        out_specs=pl.BlockSpec((1, tk, HEAD_DIM), lambda b, h, i: (b, i, h)),
        out_shape=jax.ShapeDtypeStruct((B, S, D_ATTN), BF16),
        scratch_shapes=[pltpu.VMEM((nsub, tk, tg), F32),
                        pltpu.VMEM((nsub, tk, tg), F32),
                        pltpu.VMEM((nsub, V7X_SUBLANES, tg), F32),
                        pltpu.VMEM((nsub, V7X_SUBLANES, tg), F32),
                        pltpu.VMEM((nsub, HEAD_DIM, tg), F32),
                        pltpu.VMEM((V7X_SUBLANES, V7X_LANES), F32)],
        compiler_params=_params(("parallel", "parallel", "arbitrary"), 48 << 20),
        name="fox_attn",
    )(proj, proj, kbias, vt)


def _conv_kernel(cv_ref, cg_ref, cvh_ref, cgh_ref, sx_ref, sb_ref, sc_ref, sxh_ref, sch_ref,
                 cw_ref, cb_ref, lg_ref, lb_ref, sw_ref, o_ref, pbuf, sbuf, *, ts, rows):
    first = pl.program_id(1) == 0
    nl = D_CONF // V7X_LANES
    cbuf = pbuf.at[0]
    for c in range(nl):
        cols = slice(c * V7X_LANES, (c + 1) * V7X_LANES)
        gh = cvh_ref[0, c].astype(F32) * jax.nn.sigmoid(cgh_ref[0, c].astype(F32))
        cbuf[0:CONF_HALO, cols] = jnp.where(first, 0.0, gh)
        cbuf[CONF_HALO:, cols] = cv_ref[0, c].astype(F32) * jax.nn.sigmoid(cg_ref[0, c].astype(F32))
        zh = sch_ref[0, c].astype(F32) * sxh_ref[0, c].astype(F32)
        sbuf[0:SCONV_HALO, cols] = jnp.where(first, 0.0, zh)
        sbuf[SCONV_HALO:, cols] = sc_ref[0, c].astype(F32) * sx_ref[0, c].astype(F32)

    span = ts + CONF_HALO - V7X_SUBLANES
    for s in range(1, V7X_SUBLANES):
        pbuf[s, 0:span, :] = pbuf[0, s:s + span, :]

    cb = cb_ref[0]
    lg = lg_ref[0]
    lb = lb_ref[0]
    for r in range(ts // rows):
        r0 = r * rows
        acc = jnp.zeros((rows, D_CONF), F32)
        for k in range(CONF_WIDTH):
            o = CONF_HALO - (CONF_WIDTH - 1) + k
            ph, al = o % V7X_SUBLANES, r0 + o - o % V7X_SUBLANES
            acc = acc + cw_ref[0, k:k + 1, :] * pbuf[ph, al:al + rows, :]
        acc = acc + cb
        mu = jnp.mean(acc, axis=-1, keepdims=True)
        xc = acc - mu
        var = jnp.mean(xc * xc, axis=-1, keepdims=True)
        y = xc * lax.rsqrt(var + LN_EPS) * lg + lb
        o_ref[0, r0:r0 + rows, 0:D_CONF] = (y * jax.nn.sigmoid(y)).astype(BF16)

        sacc = jnp.zeros((rows, D_SCONV), F32)
        for k in range(SCONV_WIDTH):
            o = r0 + SCONV_HALO - (SCONV_WIDTH - 1) + k
            sacc = sacc + sw_ref[0, k:k + 1, :] * sbuf[o:o + rows, :]
        for c in range(nl):
            cols = slice(c * V7X_LANES, (c + 1) * V7X_LANES)
            gated = sb_ref[0, c, r0:r0 + rows, :].astype(F32) * sacc[:, cols]
            o_ref[0, r0:r0 + rows, D_CONF + c * V7X_LANES:D_CONF + (c + 1) * V7X_LANES] = gated.astype(BF16)


def _conv_call(proj, conf_dw_w, conf_dw_b, conf_ln_g, conf_ln_b, sc_dw_w, l, *, ts=256, rows=32):
    B, _, S, _ = proj.shape
    L = conf_dw_w.shape[0]
    nl = D_CONF // V7X_LANES
    base = 2 * D_ATTN // V7X_LANES // nl

    def tile(g):
        return pl.BlockSpec((1, nl, ts, V7X_LANES), lambda b, i: (b, base + g, i, 0))

    def halo(g, n):
        return pl.BlockSpec((1, nl, n, V7X_LANES),
                            lambda b, i: (b, base + g, jnp.maximum(i * (ts // n) - 1, 0), 0))

    def wspec(k):
        return pl.BlockSpec((1, k, D_CONF), lambda b, i: (l, 0, 0))

    return pl.pallas_call(
        functools.partial(_conv_kernel, ts=ts, rows=rows),
        grid=(B, S // ts),
        in_specs=[tile(0), tile(1), halo(0, CONF_HALO), halo(1, CONF_HALO),
                  tile(2), tile(3), tile(4), halo(2, SCONV_HALO), halo(4, SCONV_HALO),
                  wspec(CONF_WIDTH), wspec(1), wspec(1), wspec(1), wspec(SCONV_WIDTH)],
        out_specs=pl.BlockSpec((1, ts, D_CONF + D_SCONV), lambda b, i: (b, i, 0)),
        out_shape=jax.ShapeDtypeStruct((B, S, D_CONF + D_SCONV), BF16),
        scratch_shapes=[pltpu.VMEM((V7X_SUBLANES, ts + CONF_HALO, D_CONF), F32),
                        pltpu.VMEM((ts + SCONV_HALO, D_SCONV), F32)],
        compiler_params=_params(("parallel", "arbitrary"), 32 << 20),
        name="conv_mixers",
    )(proj, proj, proj, proj, proj, proj, proj, proj, proj,
      conf_dw_w, conf_dw_b.reshape(L, 1, D_CONF), conf_ln_g.reshape(L, 1, D_CONF),
      conf_ln_b.reshape(L, 1, D_CONF), sc_dw_w)


def _mix_out_kernel(a_ref, c_ref, w_ref, x_ref, g_ref, o_ref):
    ka = a_ref.shape[-1]
    r = jnp.dot(a_ref[0], w_ref[0, 0:ka, :], preferred_element_type=F32)
    r = r + jnp.dot(c_ref[0], w_ref[0, ka:, :], preferred_element_type=F32)
    o_ref[0] = x_ref[0] + g_ref[0] * r


def _mix_out_call(attn, conv, w_out, x, gate, l, *, tm=512):
    B, S, D = x.shape
    ka, kc = attn.shape[-1], conv.shape[-1]
    vmem = 2 * (ka + kc) * D * 2 + 2 * tm * (ka + kc) * 2 + 4 * tm * D * 4 + 2 * tm * D * 4 + (8 << 20)
    return pl.pallas_call(
        _mix_out_kernel,
        grid=(B, S // tm),
        in_specs=[pl.BlockSpec((1, tm, ka), lambda b, i: (b, i, 0)),
                  pl.BlockSpec((1, tm, kc), lambda b, i: (b, i, 0)),
                  pl.BlockSpec((1, ka + kc, D), lambda b, i: (l, 0, 0)),
                  pl.BlockSpec((1, tm, D), lambda b, i: (b, i, 0)),
                  pl.BlockSpec((1, 1, D), lambda b, i: (b, 0, 0))],
        out_specs=pl.BlockSpec((1, tm, D), lambda b, i: (b, i, 0)),
        out_shape=jax.ShapeDtypeStruct((B, S, D), F32),
        compiler_params=_params(("parallel", "parallel"), vmem),
        name="mix_out",
    )(attn, conv, w_out, x, gate)


def _ffn_kernel(x_ref, sh_ref, sc_ref, gt_ref, g_ref, wg_ref, wv_ref, dwg_ref, dwv_ref,
                dbg_ref, dbv_ref, wd_ref, fg_ref, o_ref, h_scr, ubuf, carry_ref, *, tm, tf, final_norm):
    i = pl.program_id(1)
    j = pl.program_id(2)
    pad = V7X_SUBLANES

    @pl.when(j == 0)
    def _():
        _norm_mod_rows(x_ref, g_ref, sh_ref, sc_ref, h_scr, tm)
        o_ref[...] = jnp.zeros_like(o_ref)

    h = h_scr[...]
    th = tf // FFN_SPLIT
    halves = [slice(s * th, (s + 1) * th) for s in range(FFN_SPLIT)]
    ups = [(jnp.dot(h, wg_ref[0, :, cols], preferred_element_type=F32),
            jnp.dot(h, wv_ref[0, :, cols], preferred_element_type=F32)) for cols in halves]

    def conv(idx, u, cols, dw_ref, db_ref):
        prev = carry_ref[j, idx, :, cols]
        ubuf[idx, 0:pad, cols] = jnp.where(i == 0, 0.0, prev)
        ubuf[idx, pad:, cols] = u
        carry_ref[j, idx, :, cols] = u[tm - pad:, :]
        return (dw_ref[0, 2:3, cols] * ubuf[idx, pad:pad + tm, cols]
                + dw_ref[0, 1:2, cols] * ubuf[idx, pad - 1:pad - 1 + tm, cols]
                + dw_ref[0, 0:1, cols] * ubuf[idx, pad - 2:pad - 2 + tm, cols]
                + db_ref[0, :, cols])

    for cols, (ug, uv) in zip(halves, ups):
        yg = conv(0, ug, cols, dwg_ref, dbg_ref)
        yv = conv(1, uv, cols, dwv_ref, dbv_ref)
        act = (yg * jax.nn.sigmoid(yg) * yv).astype(BF16)
        o_ref[0] += jnp.dot(act, wd_ref[0, cols, :], preferred_element_type=F32)

    @pl.when(j == pl.num_programs(2) - 1)
    def _():
        if final_norm:
            rows = 256

            def body(r, carry):
                sl = pl.ds(pl.multiple_of(r * rows, rows), rows)
                y = x_ref[0, sl, :] + gt_ref[0] * o_ref[0, sl, :]
                ms = jnp.mean(y * y, axis=-1, keepdims=True)
                o_ref[0, sl, :] = y * lax.rsqrt(ms + RMS_EPS) * fg_ref[0]
                return carry

            lax.fori_loop(0, tm // rows, body, 0)
        else:
            o_ref[0] = x_ref[0] + gt_ref[0] * o_ref[0]


def _ffn_call(x, sh, sc, gate, g, w_up, dw_w, dw_b, w_down, final_g, l, *, final_norm, tm=1024, tf=FFN_TILE):
    B, S, D = x.shape
    L, FF, _ = w_down.shape
    nf = FF // tf
    dw_b3 = dw_b.reshape(L, 1, 2 * FF)
    vmem = 3 * tm * D * 4 + tm * D * 2 + 2 * (tm + 8) * tf * 4 \
        + 2 * 2 * D * tf * 2 + 2 * tf * D * 2 + 6 * tm * tf * 4 + (4 << 20)
    return pl.pallas_call(
        functools.partial(_ffn_kernel, tm=tm, tf=tf, final_norm=final_norm),
        grid=(B, S // tm, nf),
        in_specs=[pl.BlockSpec((1, tm, D), lambda b, i, j: (b, i, 0), pipeline_mode=pl.Buffered(1)),
                  pl.BlockSpec((1, 1, D), lambda b, i, j: (b, 0, 0)),
                  pl.BlockSpec((1, 1, D), lambda b, i, j: (b, 0, 0)),
                  pl.BlockSpec((1, 1, D), lambda b, i, j: (b, 0, 0)),
                  pl.BlockSpec((1, 1, D), lambda b, i, j: (l, 0, 0)),
                  pl.BlockSpec((1, D, tf), lambda b, i, j: (l, 0, j)),
                  pl.BlockSpec((1, D, tf), lambda b, i, j: (l, 0, nf + j)),
                  pl.BlockSpec((1, FFN_WIDTH, tf), lambda b, i, j: (l, 0, j)),
                  pl.BlockSpec((1, FFN_WIDTH, tf), lambda b, i, j: (l, 0, nf + j)),
                  pl.BlockSpec((1, 1, tf), lambda b, i, j: (l, 0, j)),
                  pl.BlockSpec((1, 1, tf), lambda b, i, j: (l, 0, nf + j)),
                  pl.BlockSpec((1, tf, D), lambda b, i, j: (l, j, 0)),
                  pl.BlockSpec((1, 1, D), lambda b, i, j: (0, 0, 0))],
        out_specs=pl.BlockSpec((1, tm, D), lambda b, i, j: (b, i, 0)),
        out_shape=jax.ShapeDtypeStruct((B, S, D), F32),
        scratch_shapes=[pltpu.VMEM((tm, D), BF16),
                        pltpu.VMEM((2, tm + V7X_SUBLANES, tf), F32),
                        pltpu.VMEM((nf, 2, V7X_SUBLANES, tf), F32)],
        compiler_params=_params(("parallel", "arbitrary", "arbitrary"), vmem),
        name="conv_ffn",
    )(x, sh, sc, gate, g, w_up, w_up, dw_w, dw_w, dw_b3, dw_b3, w_down, final_g)


def kernel(x, c, ada_w, ada_b, mix_norm_g, w_in, b_forget, conf_dw_w, conf_dw_b, conf_ln_g,
           conf_ln_b, sc_dw_w, w_out, ffn_norm_g, w_up, ffn_dw_w, ffn_dw_b, w_down, final_norm_g):
    B, S, D = x.shape
    L = ada_w.shape[0]

    f0 = 3 * D_ATTN
    v0 = 2 * D_ATTN
    w_main = jnp.concatenate([w_in[..., :v0], w_in[..., f0 + N_HEADS:], w_in[..., v0:f0]], axis=-1).astype(BF16)
    w_f = jnp.pad(w_in[..., f0:f0 + N_HEADS], ((0, 0), (0, 0), (0, V7X_LANES - N_HEADS))).astype(BF16)
    col_scale = jnp.where(jnp.arange(w_main.shape[-1]) < D_ATTN, HEAD_DIM ** -0.5 * LOG2E, 1.0)
    col_scale = col_scale.astype(F32).reshape(1, -1)
    b_f = jnp.pad(b_forget, ((0, 0), (0, V7X_LANES - N_HEADS))).reshape(L, 1, V7X_LANES)
    w_out_b = w_out.astype(BF16)
    w_up_b = w_up.astype(BF16)
    w_down_b = w_down.astype(BF16)

    mix_g = mix_norm_g.reshape(L, 1, D)
    ffn_g = ffn_norm_g.reshape(L, 1, D)
    c_pad = jnp.pad(c, ((0, V7X_SUBLANES - B), (0, 0)))
    ada = _ada_call(c_pad, ada_w, ada_b)[:, :B]
    ada = ada.reshape(L, B, N_ADA, 1, D)

    for l in range(L):
        sh_m, sc_m, g_m, sh_f, sc_f, g_f = (ada[l, :, k] for k in range(N_ADA))
        proj, vt, logf = _mix_in_call(x, sh_m, sc_m, mix_g, w_main, col_scale, w_f, b_f, l)
        kbias = _cumsum_call(logf)
        attn = _attn_call(proj, kbias, vt)
        conv = _conv_call(proj, conf_dw_w, conf_dw_b, conf_ln_g, conf_ln_b, sc_dw_w, l)
        x = _mix_out_call(attn, conv, w_out_b, x, g_m, l)
        x = _ffn_call(x, sh_f, sc_f, g_f, ffn_g, w_up_b, ffn_dw_w, ffn_dw_b, w_down_b,
                      final_norm_g.reshape(1, 1, D), l, final_norm=(l == L - 1))
    return x
```
